```python
import jax, jax.numpy as jnp
from jax import lax
import numpy as np

D_MODEL = 1024
BATCH = 2
SEQ = 8192
DEPTH = 4

HEAD_DIM = 64
PLE_DIM = 256
GRID_W = 64
ROPE_THETA = 10000.0
RMS_EPS = 1e-6
D_FF = 2816
A_HEADS = 12
A_KV_HEADS = 4
A_GROUP = A_HEADS // A_KV_HEADS
A_RADIUS = 128
B_PAIRS = ((128, 1), (512, 4), (2048, 16))
B_SLOTS = 4
B_HEADS = B_SLOTS * len(B_PAIRS)
C_HEADS = D_MODEL // HEAD_DIM
NA_KH = 8
NA_KW = 16

A_Q = A_HEADS * HEAD_DIM
A_KV = A_KV_HEADS * HEAD_DIM
B_W = B_HEADS * HEAD_DIM
AB_IN = A_Q + 2 * A_KV + 3 * B_W
AB_OUT = A_Q + B_SLOTS * HEAD_DIM
C_IN = 3 * C_HEADS * HEAD_DIM
C_OUT = C_HEADS * HEAD_DIM
N_EVEN = (DEPTH + 1) // 2
N_ODD = DEPTH // 2
NEG_INF = -1e30

kernel_name = 'hybrid_banded_dilated_neighbourhood_encoder'


def rmsnorm(x, g):
    xf = x.astype(jnp.float32)
    y = xf * lax.rsqrt(jnp.mean(xf * xf, axis=-1, keepdims=True) + RMS_EPS)
    return (y * g.astype(jnp.float32)).astype(x.dtype)


def swiglu(x, w_gate, w_up, w_down):
    return (jax.nn.silu(x @ w_gate) * (x @ w_up)) @ w_down


def rope(x, pos):
    half = x.shape[-1] // 2
    inv = ROPE_THETA ** (-jnp.arange(half, dtype=jnp.float32) / half)
    ang = pos.astype(jnp.float32)[:, None] * inv[None, :]
    cos = jnp.cos(ang)[None, :, None, :]
    sin = jnp.sin(ang)[None, :, None, :]
    xf = x.astype(jnp.float32)
    x1, x2 = xf[..., :half], xf[..., half:]
    return jnp.concatenate([x1 * cos - x2 * sin, x2 * cos + x1 * sin], axis=-1).astype(x.dtype)


def banded_attention(q, k, v, radius, sink=None):
    n, L, hk, g, dh = q.shape
    bs = radius
    nb = -(-L // bs)
    lp = nb * bs
    q = jnp.pad(q, ((0, 0), (0, lp - L), (0, 0), (0, 0), (0, 0)))
    kv_pad = ((0, 0), (bs, lp - L + bs), (0, 0), (0, 0))
    k = jnp.pad(k, kv_pad).reshape(n, nb + 2, bs, hk, dh)
    v = jnp.pad(v, kv_pad).reshape(n, nb + 2, bs, hk, dh)
    kw = jnp.concatenate([k[:, :-2], k[:, 1:-1], k[:, 2:]], axis=2)
    vw = jnp.concatenate([v[:, :-2], v[:, 1:-1], v[:, 2:]], axis=2)
    qb = q.reshape(n, nb, bs, hk, g, dh)
    qpos = jnp.arange(lp).reshape(nb, bs)
    kpos = (jnp.arange(nb)[:, None] - 1) * bs + jnp.arange(3 * bs)[None, :]
    mask = (jnp.abs(qpos[:, :, None] - kpos[:, None, :]) <= radius) & ((kpos >= 0) & (kpos < L))[:, None, :]
    s = jnp.einsum('nbqhgd,nbkhd->nbhgqk', qb, kw).astype(jnp.float32) * (dh ** -0.5)
    s = jnp.where(mask[None, :, None, None], s, NEG_INF)
    m = jnp.max(s, axis=-1)
    if sink is not None:
        m = jnp.maximum(m, sink[:, :, None])
    e = jnp.exp(s - m[..., None])
    den = jnp.sum(e, axis=-1)
    if sink is not None:
        den = den + jnp.exp(sink[:, :, None] - m)
    pr = (e / den[..., None]).astype(v.dtype)
    o = jnp.einsum('nbhgqk,nbkhd->nbqhgd', pr, vw).reshape(n, lp, hk, g, dh)[:, :L]
    lse = (m + jnp.log(den)).transpose(0, 1, 4, 2, 3).reshape(n, lp, hk, g)[:, :L]
    return o, lse


def dilated_attention(q, k, v):
    bsz, s, _, dh = q.shape
    outs, lses = [], []
    for gi, (window, dil) in enumerate(B_PAIRS):
        lo = gi * B_SLOTS
        sub = s // dil

        def to_res(t):
            t = t[:, :, lo:lo + B_SLOTS]
            return t.reshape(bsz, sub, dil, B_SLOTS, dh).transpose(0, 2, 1, 3, 4).reshape(bsz * dil, sub, B_SLOTS, dh)

        o, lse = banded_attention(to_res(q)[:, :, :, None], to_res(k), to_res(v), window // (2 * dil))
        outs.append(o[:, :, :, 0].reshape(bsz, dil, sub, B_SLOTS, dh).transpose(0, 2, 1, 3, 4).reshape(bsz, s, B_SLOTS, dh))
        lses.append(lse[..., 0].reshape(bsz, dil, sub, B_SLOTS).transpose(0, 2, 1, 3).reshape(bsz, s, B_SLOTS))
    wts = jax.nn.softmax(jnp.stack(lses, axis=0), axis=0)
    out = jnp.einsum('gbsh,gbshd->bshd', wts, jnp.stack(outs, axis=0).astype(jnp.float32))
    return out.astype(q.dtype)


def neighbourhood_attention(q, k, v, rpb):
    bsz, s, nh, dh = q.shape
    rows = s // GRID_W
    kh = min(NA_KH, rows)
    kw = NA_KW
    qg = q.reshape(bsz, rows, GRID_W, nh, dh)
    kg = k.reshape(bsz, rows, GRID_W, nh, dh)
    vg = v.reshape(bsz, rows, GRID_W, nh, dh)
    cols = jnp.arange(GRID_W)
    col_start = jnp.clip(cols - kw // 2, 0, GRID_W - kw)
    col_idx = col_start[:, None] + jnp.arange(kw)[None, :]
    dc = col_idx - cols[:, None] + (NA_KW - 1)
    scale = dh ** -0.5

    def row_fn(i):
        rs = jnp.clip(i - kh // 2, 0, rows - kh)
        k_win = lax.dynamic_slice_in_dim(kg, rs, kh, axis=1)[:, :, col_idx]
        v_win = lax.dynamic_slice_in_dim(vg, rs, kh, axis=1)[:, :, col_idx]
        q_row = lax.dynamic_index_in_dim(qg, i, axis=1, keepdims=False)
        sc = jnp.einsum('bjhd,bajkhd->bhjak', q_row, k_win).astype(jnp.float32) * scale
        dr = rs + jnp.arange(kh) - i + (NA_KH - 1)
        bias = rpb[:, dr][:, :, dc].transpose(0, 2, 1, 3).astype(jnp.float32)
        sc = (sc + bias[None]).reshape(bsz, nh, GRID_W, kh * kw)
        pr = jax.nn.softmax(sc, axis=-1).reshape(bsz, nh, GRID_W, kh, kw).astype(v.dtype)
        return jnp.einsum('bhjak,bajkhd->bjhd', pr, v_win)

    o = lax.map(row_fn, jnp.arange(rows))
    return o.transpose(1, 0, 2, 3, 4).reshape(bsz, s, nh, dh)


def ab_mixer(xn, w_in, sink, w_out, pos):
    bsz, s, _ = xn.shape
    splits = [A_Q, A_Q + A_KV, A_Q + 2 * A_KV, A_Q + 2 * A_KV + B_W, A_Q + 2 * A_KV + 2 * B_W]
    qa, ka, va, qb, kb, vb = jnp.split(xn @ w_in, splits, axis=-1)
    qa = rope(qa.reshape(bsz, s, A_HEADS, HEAD_DIM), pos).reshape(bsz, s, A_KV_HEADS, A_GROUP, HEAD_DIM)
    ka = rope(ka.reshape(bsz, s, A_KV_HEADS, HEAD_DIM), pos)
    va = va.reshape(bsz, s, A_KV_HEADS, HEAD_DIM)
    oa, _ = banded_attention(qa, ka, va, A_RADIUS, sink.astype(jnp.float32).reshape(A_KV_HEADS, A_GROUP))
    shp = (bsz, s, B_HEADS, HEAD_DIM)
    ob = dilated_attention(rope(qb.reshape(shp), pos), rope(kb.reshape(shp), pos), vb.reshape(shp))
    mixed = jnp.concatenate([oa.reshape(bsz, s, A_Q).astype(xn.dtype),
                             ob.reshape(bsz, s, B_SLOTS * HEAD_DIM).astype(xn.dtype)], axis=-1)
    return mixed @ w_out


def c_mixer(xn, w_in, rpb, w_out):
    bsz, s, _ = xn.shape
    q, k, v = jnp.split(xn @ w_in, 3, axis=-1)
    shp = (bsz, s, C_HEADS, HEAD_DIM)
    o = neighbourhood_attention(q.reshape(shp), k.reshape(shp), v.reshape(shp), rpb)
    return o.reshape(bsz, s, C_OUT) @ w_out


def setup_inputs(seed: int = 0) -> dict:
    key = jax.random.key(seed)
    ks = jax.random.split(key, 21)
    d = D_MODEL

    def nrm(k, shape, scale):
        return scale * jax.random.normal(k, shape, jnp.float32)

    def gain(k, shape):
        return 1.0 + 0.05 * jax.random.normal(k, shape, jnp.float32)

    return {
        'x': nrm(ks[0], (BATCH, SEQ, d), 1.0),
        'p': nrm(ks[1], (DEPTH, BATCH, SEQ, PLE_DIM), 1.0),
        'norm_ffn1': gain(ks[2], (DEPTH, d)),
        'ffn1_w_gate': nrm(ks[3], (DEPTH, d, D_FF), d ** -0.5),
        'ffn1_w_up': nrm(ks[4], (DEPTH, d, D_FF), d ** -0.5),
        'ffn1_w_down': nrm(ks[5], (DEPTH, D_FF, d), D_FF ** -0.5),
        'norm_mix': gain(ks[6], (DEPTH, d)),
        'w_in_ab': nrm(ks[7], (N_EVEN, d, AB_IN), d ** -0.5),
        'sink_a': nrm(ks[8], (N_EVEN, A_HEADS), 0.5),
        'w_out_ab': nrm(ks[9], (N_EVEN, AB_OUT, d), AB_OUT ** -0.5),
        'w_in_c': nrm(ks[10], (N_ODD, d, C_IN), d ** -0.5),
        'rpb_c': nrm(ks[11], (N_ODD, C_HEADS, 2 * NA_KH - 1, 2 * NA_KW - 1), 0.5),
        'w_out_c': nrm(ks[12], (N_ODD, C_OUT, d), C_OUT ** -0.5),
        'norm_ffn2': gain(ks[13], (DEPTH, d)),
        'ffn2_w_gate': nrm(ks[14], (DEPTH, d, D_FF), d ** -0.5),
        'ffn2_w_up': nrm(ks[15], (DEPTH, d, D_FF), d ** -0.5),
        'ffn2_w_down': nrm(ks[16], (DEPTH, D_FF, d), D_FF ** -0.5),
        'norm_ple': gain(ks[17], (DEPTH, d)),
        'w_ple_gate': nrm(ks[18], (DEPTH, d, d), d ** -0.5),
        'w_ple_proj': nrm(ks[19], (DEPTH, PLE_DIM, d), PLE_DIM ** -0.5),
        'norm_final': gain(ks[20], (d,)),
    }


def reference(x, p, norm_ffn1, ffn1_w_gate, ffn1_w_up, ffn1_w_down, norm_mix, w_in_ab, sink_a,
              w_out_ab, w_in_c, rpb_c, w_out_c, norm_ffn2, ffn2_w_gate, ffn2_w_up, ffn2_w_down,
              norm_ple, w_ple_gate, w_ple_proj, norm_final):
    s = x.shape[1]
    pos = jnp.arange(s, dtype=jnp.int32)
    h = x
    for i in range(DEPTH):
        h = h + 0.5 * swiglu(rmsnorm(h, norm_ffn1[i]), ffn1_w_gate[i], ffn1_w_up[i], ffn1_w_down[i])
        hn = rmsnorm(h, norm_mix[i])
        j = i // 2
        if i % 2 == 0:
            h = h + ab_mixer(hn, w_in_ab[j], sink_a[j], w_out_ab[j], pos)
        else:
            h = h + c_mixer(hn, w_in_c[j], rpb_c[j], w_out_c[j])
        h = h + 0.5 * swiglu(rmsnorm(h, norm_ffn2[i]), ffn2_w_gate[i], ffn2_w_up[i], ffn2_w_down[i])
        gate = jax.nn.sigmoid(rmsnorm(h, norm_ple[i]) @ w_ple_gate[i])
        h = h + gate * (p[i] @ w_ple_proj[i])
    return rmsnorm(h, norm_final)
```

```python
import functools

import jax
import jax.numpy as jnp
from jax import lax
from jax.experimental import pallas as pl
from jax.experimental.pallas import tpu as pltpu

HEAD_DIM = 64
GRID_W = 64
ROPE_THETA = 10000.0
RMS_EPS = 1e-6
A_HEADS = 12
A_KV_HEADS = 4
A_RADIUS = 128
B_PAIRS = ((128, 1), (512, 4), (2048, 16))
B_SLOTS = 4
NA_KH = 8
NA_KW = 16
NEG_INF = -1e30

LANES = 128
ATT_BLOCK = 128
VMEM_LIMIT = 56 * 1024 * 1024
F32 = jnp.float32
BF16 = jnp.bfloat16


def _params(n_axes):
    return pltpu.CompilerParams(dimension_semantics=("arbitrary",) * n_axes,
                                vmem_limit_bytes=VMEM_LIMIT)


def _resident(block_shape, index_map):
    return pl.BlockSpec(block_shape, index_map, pipeline_mode=pl.Buffered(1))


def _rms(x, g):
    ms = jnp.mean(x * x, axis=-1, keepdims=True)
    return x * lax.rsqrt(ms + RMS_EPS) * g


def _ffn_kernel(h_ref, g_ref, wg_ref, wu_ref, wd_ref, o_ref, a_ref, *, tf):
    x = h_ref[...]
    xn = _rms(x, g_ref[...]).astype(BF16)
    for f in range(wg_ref.shape[1] // tf):
        sl = slice(f * tf, (f + 1) * tf)
        gate = jnp.dot(xn, wg_ref[:, sl], preferred_element_type=F32)
        up = jnp.dot(xn, wu_ref[:, sl], preferred_element_type=F32)
        a_ref[:, sl] = (gate * jax.nn.sigmoid(gate) * up).astype(BF16)
    y = jnp.dot(a_ref[...], wd_ref[...], preferred_element_type=F32)
    o_ref[...] = x + 0.5 * y


def _ffn(h, gain, w_gate, w_up, w_down, layer, *, tm=512, tf=256):
    t, d = h.shape
    f = w_gate.shape[-1]
    row = lambda i: (i, 0)
    return pl.pallas_call(
        functools.partial(_ffn_kernel, tf=tf),
        grid=(t // tm,),
        in_specs=[
            pl.BlockSpec((tm, d), row),
            _resident((None, 1, d), lambda i: (layer, 0, 0)),
            _resident((None, d, f), lambda i: (layer, 0, 0)),
            _resident((None, d, f), lambda i: (layer, 0, 0)),
            _resident((None, f, d), lambda i: (layer, 0, 0)),
        ],
        out_specs=pl.BlockSpec((tm, d), row),
        out_shape=jax.ShapeDtypeStruct((t, d), F32),
        scratch_shapes=[pltpu.VMEM((tm, f), BF16)],
        compiler_params=_params(1),
        name="ffn",
    )(h, gain, w_gate, w_up, w_down)


def _rope(y, cos, sin_signed):
    lane = lax.broadcasted_iota(jnp.int32, y.shape, 1)
    first_half = (lane & (HEAD_DIM // 2)) == 0
    swapped = jnp.where(first_half, pltpu.roll(y, LANES - HEAD_DIM // 2, 1),
                        pltpu.roll(y, HEAD_DIM // 2, 1))
    return y * cos + swapped * sin_signed


_AB_QA, _AB_KA, _AB_VA, _AB_B = 768, 256, 256, 2304


def _inproj_ab_kernel(h_ref, g_ref, w_ref, cos_ref, sin_ref, qa_ref, ka_ref, va_ref, b_ref, *, tn):
    xn = _rms(h_ref[...], g_ref[...]).astype(BF16)
    cos = cos_ref[...]
    sin = sin_ref[...]
    scale = HEAD_DIM ** -0.5
    b_base = _AB_QA + _AB_KA + _AB_VA
    b_w = _AB_B // 3
    for c in range(w_ref.shape[1] // tn):
        col = c * tn
        y = jnp.dot(xn, w_ref[:, col:col + tn], preferred_element_type=F32)
        is_v = (_AB_QA + _AB_KA <= col < b_base) or col >= b_base + 2 * b_w
        is_q = col < _AB_QA or (b_base <= col < b_base + b_w)
        if not is_v:
            y = jnp.concatenate(
                [_rope(y[:, j:j + LANES], cos, sin) for j in range(0, tn, LANES)], axis=1)
        if is_q:
            y = y * scale
        y = y.astype(BF16)
        if col < _AB_QA:
            qa_ref[:, col:col + tn] = y
        elif col < _AB_QA + _AB_KA:
            ka_ref[:, col - _AB_QA:col - _AB_QA + tn] = y
        elif col < b_base:
            va_ref[:, col - _AB_QA - _AB_KA:col - _AB_QA - _AB_KA + tn] = y
        else:
            b_ref[:, col - b_base:col - b_base + tn] = y


def _inproj_ab(h, gain, w_in, cos, sin, layer, j, *, tm=512, tn=256):
    t, d = h.shape
    s = cos.shape[0]
    n = w_in.shape[-1]
    row = lambda i: (i, 0)
    tab = lambda i: (i % (s // tm), 0)
    outs = (_AB_QA, _AB_KA, _AB_VA, _AB_B)
    return pl.pallas_call(
        functools.partial(_inproj_ab_kernel, tn=tn),
        grid=(t // tm,),
        in_specs=[
            pl.BlockSpec((tm, d), row),
            _resident((None, 1, d), lambda i: (layer, 0, 0)),
            _resident((None, d, n), lambda i: (j, 0, 0)),
            pl.BlockSpec((tm, LANES), tab),
            pl.BlockSpec((tm, LANES), tab),
        ],
        out_specs=[pl.BlockSpec((tm, w), row) for w in outs],
        out_shape=[jax.ShapeDtypeStruct((t, w), BF16) for w in outs],
        compiler_params=_params(1),
        name="inproj_ab",
    )(h, gain, w_in, cos, sin)


def _band_kernel(*refs, radius, hk, group, length, use_sink, want_lse):
    refs = list(refs)
    sink_ref = refs.pop(0) if use_sink else None
    q_ref, kp_ref, kc_ref, kn_ref, vp_ref, vc_ref, vn_ref = refs[:7]
    o_ref = refs[7]
    lse_ref = refs[8] if want_lse else None
    t = pl.program_id(2)
    tq = q_ref.shape[0]
    k = jnp.concatenate([kp_ref[...], kc_ref[...], kn_ref[...]], axis=0)
    v = jnp.concatenate([vp_ref[...], vc_ref[...], vn_ref[...]], axis=0)
    qpos = t * tq + lax.broadcasted_iota(jnp.int32, (tq, 3 * tq), 0)
    kpos = (t - 1) * tq + lax.broadcasted_iota(jnp.int32, (tq, 3 * tq), 1)
    mask = (jnp.abs(qpos - kpos) <= radius) & (kpos >= 0) & (kpos < length)
    for kv in range(hk):
        kh = k[:, kv * HEAD_DIM:(kv + 1) * HEAD_DIM]
        vh = v[:, kv * HEAD_DIM:(kv + 1) * HEAD_DIM]
        for g in range(group):
            h = kv * group + g
            cols = slice(h * HEAD_DIM, (h + 1) * HEAD_DIM)
            s = lax.dot_general(q_ref[:, cols], kh, (((1,), (1,)), ((), ())),
                                preferred_element_type=F32)
            s = jnp.where(mask, s, NEG_INF)
            m = jnp.max(s, axis=-1, keepdims=True)
            if use_sink:
                m = jnp.maximum(m, sink_ref[h])
            e = jnp.exp(s - m)
            den = jnp.sum(e, axis=-1, keepdims=True)
            if use_sink:
                den = den + jnp.exp(sink_ref[h] - m)
            o = jnp.dot(e.astype(BF16), vh, preferred_element_type=F32) / den
            o_ref[:, cols] = o.astype(o_ref.dtype)
            if want_lse:
                lse_ref[:, cols] = jnp.broadcast_to(m + jnp.log(den), (tq, HEAD_DIM))


def _banded_attention(q, k, v, *, q_col, k_col, v_col, dil, radius, hk, group, out_cols, sink=None,
                      want_lse=False):
    bsz, length, _ = q.shape
    tq = ATT_BLOCK
    nt = length // tq
    qw = hk * group * HEAD_DIM
    kw = hk * HEAD_DIM
    q_stride = q.shape[-1] // dil // qw
    k_stride = k.shape[-1] // dil // kw
    v_stride = v.shape[-1] // dil // kw
    prev = lambda t: jnp.maximum(t - 1, 0)
    nxt = lambda t: jnp.minimum(t + 1, nt - 1)
    in_specs = [
        pl.BlockSpec((None, tq, qw), lambda b, r, t: (b, t, r * q_stride + q_col)),
        pl.BlockSpec((None, tq, kw), lambda b, r, t: (b, prev(t), r * k_stride + k_col)),
        pl.BlockSpec((None, tq, kw), lambda b, r, t: (b, t, r * k_stride + k_col)),
        pl.BlockSpec((None, tq, kw), lambda b, r, t: (b, nxt(t), r * k_stride + k_col)),
        pl.BlockSpec((None, tq, kw), lambda b, r, t: (b, prev(t), r * v_stride + v_col)),
        pl.BlockSpec((None, tq, kw), lambda b, r, t: (b, t, r * v_stride + v_col)),
        pl.BlockSpec((None, tq, kw), lambda b, r, t: (b, nxt(t), r * v_stride + v_col)),
    ]
    args = [q, k, k, k, v, v, v]
    if sink is not None:
        in_specs.insert(0, pl.BlockSpec(memory_space=pltpu.SMEM))
        args.insert(0, sink)
    out_map = lambda b, r, t: (b, t, r)
    out_specs = [pl.BlockSpec((None, tq, out_cols), out_map)]
    out_shape = [jax.ShapeDtypeStruct((bsz, length, dil * out_cols), BF16)]
    if want_lse:
        out_specs.append(pl.BlockSpec((None, tq, out_cols), out_map))
        out_shape.append(jax.ShapeDtypeStruct((bsz, length, dil * out_cols), F32))
    return pl.pallas_call(
        functools.partial(_band_kernel, radius=radius, hk=hk, group=group, length=length,
                          use_sink=sink is not None, want_lse=want_lse),
        grid=(bsz, dil, nt),
        in_specs=in_specs,
        out_specs=out_specs,
        out_shape=out_shape,
        compiler_params=_params(3),
        name=f"band_r{radius}_d{dil}",
    )(*args)


def _outproj_ab_kernel(h_ref, oa_ref, o0_ref, o1_ref, o2_ref, l0_ref, l1_ref, l2_ref, w_ref, out_ref):
    l0, l1, l2 = l0_ref[...], l1_ref[...], l2_ref[...]
    m = jnp.maximum(jnp.maximum(l0, l1), l2)
    e0, e1, e2 = jnp.exp(l0 - m), jnp.exp(l1 - m), jnp.exp(l2 - m)
    ob = (e0 * o0_ref[...].astype(F32) + e1 * o1_ref[...].astype(F32)
          + e2 * o2_ref[...].astype(F32)) / (e0 + e1 + e2)
    na = oa_ref.shape[1]
    y = jnp.dot(oa_ref[...], w_ref[:na, :], preferred_element_type=F32)
    y = y + jnp.dot(ob.astype(BF16), w_ref[na:, :], preferred_element_type=F32)
    out_ref[...] = h_ref[...] + y


def _outproj_ab(h, oa, obs, lses, w_out, j, *, tm=512):
    t, d = h.shape
    row = lambda i: (i, 0)
    nb = obs[0].shape[1]
    return pl.pallas_call(
        _outproj_ab_kernel,
        grid=(t // tm,),
        in_specs=[pl.BlockSpec((tm, d), row), pl.BlockSpec((tm, oa.shape[1]), row)]
        + [pl.BlockSpec((tm, nb), row)] * 6
        + [_resident((None,) + w_out.shape[1:], lambda i: (j, 0, 0))],
        out_specs=pl.BlockSpec((tm, d), row),
        out_shape=jax.ShapeDtypeStruct((t, d), F32),
        compiler_params=_params(1),
        name="outproj_ab",
    )(h, oa, *obs, *lses, w_out)


def _inproj_c_kernel(h_ref, g_ref, w_ref, q_ref, k_ref, v_ref, *, tn):
    xn = _rms(h_ref[...], g_ref[...]).astype(BF16)
    width = q_ref.shape[1]
    scale = HEAD_DIM ** -0.5
    for c in range(w_ref.shape[1] // tn):
        col = c * tn
        y = jnp.dot(xn, w_ref[:, col:col + tn], preferred_element_type=F32)
        if col < width:
            q_ref[:, col:col + tn] = (y * scale).astype(BF16)
        elif col < 2 * width:
            k_ref[:, col - width:col - width + tn] = y.astype(BF16)
        else:
            v_ref[:, col - 2 * width:col - 2 * width + tn] = y.astype(BF16)


def _inproj_c(h, gain, w_in, layer, j, *, tm=512, tn=256):
    t, d = h.shape
    n = w_in.shape[-1]
    row = lambda i: (i, 0)
    return pl.pallas_call(
        functools.partial(_inproj_c_kernel, tn=tn),
        grid=(t // tm,),
        in_specs=[
            pl.BlockSpec((tm, d), row),
            _resident((None, 1, d), lambda i: (layer, 0, 0)),
            _resident((None, d, n), lambda i: (j, 0, 0)),
        ],
        out_specs=[pl.BlockSpec((tm, n // 3), row)] * 3,
        out_shape=[jax.ShapeDtypeStruct((t, n // 3), BF16)] * 3,
        compiler_params=_params(1),
        name="inproj_c",
    )(h, gain, w_in)


def _na_bias_kernel(rpb_ref, o_ref):
    h = pl.program_id(0)
    n_dr, n_dc = rpb_ref.shape[1], rpb_ref.shape[2]
    j = lax.broadcasted_iota(jnp.int32, (GRID_W, GRID_W), 0)
    c = lax.broadcasted_iota(jnp.int32, (GRID_W, GRID_W), 1)
    start = jnp.clip(j - NA_KW // 2, 0, GRID_W - NA_KW)
    inside = (c >= start) & (c < start + NA_KW)
    dc = c - j + (NA_KW - 1)
    for dr in range(n_dr):
        acc = jnp.full((GRID_W, GRID_W), NEG_INF, F32)
        for t in range(n_dc):
            acc = jnp.where(dc == t, rpb_ref[h, dr, t], acc)
        o_ref[dr] = jnp.where(inside, acc, NEG_INF)


def _na_bias(rpb):
    nh, n_dr, _ = rpb.shape
    return pl.pallas_call(
        _na_bias_kernel,
        grid=(nh,),
        in_specs=[pl.BlockSpec(memory_space=pltpu.SMEM)],
        out_specs=pl.BlockSpec((None, n_dr, GRID_W, GRID_W), lambda h: (h, 0, 0, 0)),
        out_shape=jax.ShapeDtypeStruct((nh, n_dr, GRID_W, GRID_W), F32),
        compiler_params=_params(1),
        name="na_bias",
    )(rpb)


def _na_kernel(q_ref, k_ref, v_ref, tb_ref, o_ref, *, rows_per_step, rows, kh):
    ig = pl.program_id(2)
    left = lax.broadcasted_iota(jnp.int32, (GRID_W, LANES), 1) < HEAD_DIM
    zero = jnp.zeros((GRID_W, LANES), BF16)

    def body(g, carry):
        i = ig * rows_per_step + g
        rs = jnp.clip(i - kh // 2, 0, rows - kh)
        off = rs - i + (NA_KH - 1)
        start = pl.multiple_of(rs * GRID_W, GRID_W)
        kwin = k_ref[pl.ds(start, kh * GRID_W), :]
        vwin = v_ref[pl.ds(start, kh * GRID_W), :]
        qrow = pl.ds(pl.multiple_of(g * GRID_W, GRID_W), GRID_W)
        qi = q_ref[qrow, :]
        q2 = jnp.concatenate([jnp.where(left, qi, zero), jnp.where(left, zero, qi)], axis=0)
        s = lax.dot_general(q2, kwin, (((1,), (1,)), ((), ())), preferred_element_type=F32)
        s = s + tb_ref[off]
        m = jnp.max(s, axis=-1, keepdims=True)
        e = jnp.exp(s - m)
        den = jnp.sum(e, axis=-1, keepdims=True)
        o2 = jnp.dot(e.astype(BF16), vwin, preferred_element_type=F32) / den
        o_ref[qrow, :] = jnp.where(left, o2[:GRID_W], o2[GRID_W:]).astype(o_ref.dtype)
        return carry

    lax.fori_loop(0, rows_per_step, body, 0)


def _neighbourhood_attention(q, k, v, bias, *, rows_per_step=8):
    bsz, s, width = q.shape
    rows = s // GRID_W
    kh = min(NA_KH, rows)
    n_pairs = width // LANES
    g = rows_per_step
    return pl.pallas_call(
        functools.partial(_na_kernel, rows_per_step=g, rows=rows, kh=kh),
        grid=(bsz, n_pairs, rows // g),
        in_specs=[
            pl.BlockSpec((None, g * GRID_W, LANES), lambda b, p, i: (b, i, p)),
            pl.BlockSpec((None, s, LANES), lambda b, p, i: (b, 0, p)),
            pl.BlockSpec((None, s, LANES), lambda b, p, i: (b, 0, p)),
            pl.BlockSpec((bias.shape[0], None) + bias.shape[2:], lambda b, p, i: (0, p, 0, 0)),
        ],
        out_specs=pl.BlockSpec((None, g * GRID_W, LANES), lambda b, p, i: (b, i, p)),
        out_shape=jax.ShapeDtypeStruct((bsz, s, width), BF16),
        compiler_params=_params(3),
        name="na_attention",
    )(q, k, v, bias)


def _outproj_c_kernel(h_ref, o_ref, w_ref, out_ref):
    out_ref[...] = h_ref[...] + jnp.dot(o_ref[...], w_ref[...], preferred_element_type=F32)


def _outproj_c(h, o, w_out, j, *, tm=512):
    t, d = h.shape
    row = lambda i: (i, 0)
    return pl.pallas_call(
        _outproj_c_kernel,
        grid=(t // tm,),
        in_specs=[pl.BlockSpec((tm, d), row), pl.BlockSpec((tm, o.shape[1]), row),
                  _resident((None,) + w_out.shape[1:], lambda i: (j, 0, 0))],
        out_specs=pl.BlockSpec((tm, d), row),
        out_shape=jax.ShapeDtypeStruct((t, d), F32),
        compiler_params=_params(1),
        name="outproj_c",
    )(h, o, w_out)


def _ple_kernel(h_ref, g_ref, wg_ref, p_ref, wp_ref, gf_ref, o_ref, *, final):
    x = h_ref[...]
    xn = _rms(x, g_ref[...]).astype(BF16)
    gate = jax.nn.sigmoid(jnp.dot(xn, wg_ref[...], preferred_element_type=F32))
    proj = jnp.dot(p_ref[...].astype(BF16), wp_ref[...], preferred_element_type=F32)
    y = x + gate * proj
    o_ref[...] = _rms(y, gf_ref[...]) if final else y


def _ple(h, gain, w_gate, p, w_proj, gain_final, layer, *, final, tm=512):
    t, d = h.shape
    pd = p.shape[-1]
    row = lambda i: (i, 0)
    return pl.pallas_call(
        functools.partial(_ple_kernel, final=final),
        grid=(t // tm,),
        in_specs=[
            pl.BlockSpec((tm, d), row),
            _resident((None, 1, d), lambda i: (layer, 0, 0)),
            _resident((None, d, d), lambda i: (layer, 0, 0)),
            pl.BlockSpec((None, tm, pd), lambda i: (layer, i, 0)),
            _resident((None, pd, d), lambda i: (layer, 0, 0)),
            _resident((1, d), lambda i: (0, 0)),
        ],
        out_specs=pl.BlockSpec((tm, d), row),
        out_shape=jax.ShapeDtypeStruct((t, d), F32),
        compiler_params=_params(1),
        name="ple",
    )(h, gain, w_gate, p, w_proj, gain_final)


def _rope_tables(s):
    half = HEAD_DIM // 2
    inv = ROPE_THETA ** (-jnp.arange(half, dtype=F32) / half)
    ang = jnp.arange(s, dtype=jnp.int32).astype(F32)[:, None] * inv[None, :]
    cos, sin = jnp.cos(ang), jnp.sin(ang)
    reps = LANES // HEAD_DIM
    return (jnp.tile(jnp.concatenate([cos, cos], axis=1), (1, reps)),
            jnp.tile(jnp.concatenate([-sin, sin], axis=1), (1, reps)))


def _ab_mixer(h, gain, w_in, sink, w_out, cos, sin, layer, j, bsz, s):
    qa, ka, va, qkvb = _inproj_ab(h, gain, w_in, cos, sin, layer, j)
    a_group = A_HEADS // A_KV_HEADS
    oa = _banded_attention(
        qa.reshape(bsz, s, -1), ka.reshape(bsz, s, -1), va.reshape(bsz, s, -1),
        q_col=0, k_col=0, v_col=0, dil=1, radius=A_RADIUS, hk=A_KV_HEADS, group=a_group,
        out_cols=A_HEADS * HEAD_DIM, sink=sink[j])[0]
    obs, lses = [], []
    n_groups = len(B_PAIRS)
    for gi, (window, dil) in enumerate(B_PAIRS):
        view = qkvb.reshape(bsz, s // dil, dil * qkvb.shape[-1])
        o, lse = _banded_attention(
            view, view, view, q_col=gi, k_col=n_groups + gi, v_col=2 * n_groups + gi, dil=dil,
            radius=window // (2 * dil), hk=B_SLOTS, group=1, out_cols=B_SLOTS * HEAD_DIM,
            want_lse=True)
        obs.append(o.reshape(bsz * s, -1))
        lses.append(lse.reshape(bsz * s, -1))
    return _outproj_ab(h, oa.reshape(bsz * s, -1), obs, lses, w_out, j)


def _c_mixer(h, gain, w_in, rpb, w_out, layer, j, bsz, s):
    q, k, v = _inproj_c(h, gain, w_in, layer, j)
    toep = _na_bias(rpb[j])
    nh = toep.shape[0]
    kh = min(NA_KH, s // GRID_W)
    bias = jnp.stack([
        jnp.concatenate([toep[:, off + a] for a in range(kh)], axis=-1) for off in range(NA_KH)])
    bias = bias.reshape(NA_KH, nh // 2, 2 * GRID_W, kh * GRID_W)
    o = _neighbourhood_attention(q.reshape(bsz, s, -1), k.reshape(bsz, s, -1),
                                 v.reshape(bsz, s, -1), bias)
    return _outproj_c(h, o.reshape(bsz * s, -1), w_out, j)


def kernel(x, p, norm_ffn1, ffn1_w_gate, ffn1_w_up, ffn1_w_down, norm_mix, w_in_ab, sink_a, w_out_ab, w_in_c, rpb_c, w_out_c, norm_ffn2, ffn2_w_gate, ffn2_w_up, ffn2_w_down, norm_ple, w_ple_gate, w_ple_proj, norm_final):
    bsz, s, d = x.shape
    depth = p.shape[0]
    bf = lambda w: w.astype(BF16)
    gains = lambda g: g.reshape(g.shape[0], 1, d)
    ffn1 = (gains(norm_ffn1), bf(ffn1_w_gate), bf(ffn1_w_up), bf(ffn1_w_down))
    ffn2 = (gains(norm_ffn2), bf(ffn2_w_gate), bf(ffn2_w_up), bf(ffn2_w_down))
    g_mix, g_ple = gains(norm_mix), gains(norm_ple)
    w_in_ab, w_out_ab, w_in_c, w_out_c = bf(w_in_ab), bf(w_out_ab), bf(w_in_c), bf(w_out_c)
    w_ple_gate, w_ple_proj = bf(w_ple_gate), bf(w_ple_proj)
    p2 = p.reshape(depth, bsz * s, p.shape[-1])
    g_final = norm_final.reshape(1, d)
    cos, sin = _rope_tables(s)

    h = x.reshape(bsz * s, d)
    for i in range(depth):
        h = _ffn(h, *ffn1, i)
        j = i // 2
        if i % 2 == 0:
            h = _ab_mixer(h, g_mix, w_in_ab, sink_a, w_out_ab, cos, sin, i, j, bsz, s)
        else:
            h = _c_mixer(h, g_mix, w_in_c, rpb_c, w_out_c, i, j, bsz, s)
        h = _ffn(h, *ffn2, i)
        h = _ple(h, g_ple, w_ple_gate, p2, w_ple_proj, g_final, i, final=i == depth - 1)
    return h.reshape(bsz, s, d)
```

```python
import functools

import jax
import jax.numpy as jnp
from jax import lax
from jax.experimental import pallas as pl
from jax.experimental.pallas import tpu as pltpu

HEAD_DIM = 64
GRID_W = 64
ROPE_THETA = 10000.0
RMS_EPS = 1e-6
A_HEADS = 12
A_KV_HEADS = 4
A_RADIUS = 128
B_PAIRS = ((128, 1), (512, 4), (2048, 16))
B_SLOTS = 4
NA_KH = 8
NA_KW = 16
NEG_INF = -1e30

LANES = 128
ATT_BLOCK = 128
VMEM_LIMIT = 56 * 1024 * 1024
F32 = jnp.float32
BF16 = jnp.bfloat16


def _params(n_axes):
    return pltpu.CompilerParams(dimension_semantics=("arbitrary",) * n_axes,
                                vmem_limit_bytes=VMEM_LIMIT)


def _resident(block_shape, index_map):
    return pl.BlockSpec(block_shape, index_map, pipeline_mode=pl.Buffered(1))


def _rms(x, g):
    ms = jnp.mean(x * x, axis=-1, keepdims=True)
    return x * lax.rsqrt(ms + RMS_EPS) * g


def _ffn_kernel(h_ref, g_ref, wg_ref, wu_ref, wd_ref, o_ref, a_ref, *, tf):
    x = h_ref[...]
    xn = _rms(x, g_ref[...]).astype(BF16)
    for f in range(wg_ref.shape[1] // tf):
        sl = slice(f * tf, (f + 1) * tf)
        gate = jnp.dot(xn, wg_ref[:, sl], preferred_element_type=F32)
        up = jnp.dot(xn, wu_ref[:, sl], preferred_element_type=F32)
        a_ref[:, sl] = (gate * jax.nn.sigmoid(gate) * up).astype(BF16)
    y = jnp.dot(a_ref[...], wd_ref[...], preferred_element_type=F32)
    o_ref[...] = x + 0.5 * y


def _ffn(h, gain, w_gate, w_up, w_down, layer, *, tm=512, tf=256):
    t, d = h.shape
    f = w_gate.shape[-1]
    row = lambda i: (i, 0)
    return pl.pallas_call(
        functools.partial(_ffn_kernel, tf=tf),
        grid=(t // tm,),
        in_specs=[
            pl.BlockSpec((tm, d), row),
            _resident((None, 1, d), lambda i: (layer, 0, 0)),
            _resident((None, d, f), lambda i: (layer, 0, 0)),
            _resident((None, d, f), lambda i: (layer, 0, 0)),
            _resident((None, f, d), lambda i: (layer, 0, 0)),
        ],
        out_specs=pl.BlockSpec((tm, d), row),
        out_shape=jax.ShapeDtypeStruct((t, d), F32),
        scratch_shapes=[pltpu.VMEM((tm, f), BF16)],
        compiler_params=_params(1),
        name="ffn",
    )(h, gain, w_gate, w_up, w_down)


def _rope(y, cos, sin_signed):
    lane = lax.broadcasted_iota(jnp.int32, y.shape, 1)
    first_half = (lane & (HEAD_DIM // 2)) == 0
    swapped = jnp.where(first_half, pltpu.roll(y, LANES - HEAD_DIM // 2, 1),
                        pltpu.roll(y, HEAD_DIM // 2, 1))
    return y * cos + swapped * sin_signed


A_GROUP = A_HEADS // A_KV_HEADS
A_HEAD_ORDER = (0, 3, 1, 4, 2, 5, 6, 9, 7, 10, 8, 11)
_A_Q, _A_KV = A_HEADS * HEAD_DIM, A_KV_HEADS * HEAD_DIM
_B_W = B_SLOTS * HEAD_DIM


def _scatter_residues(y, buf_ref, out_ref, dil, col):
    n = y.shape[0] // dil
    slabs = y.shape[1] // LANES
    for sl in range(slabs):
        buf_ref[sl] = y[:, sl * LANES:(sl + 1) * LANES]
    for r in range(dil):
        for sl in range(slabs):
            piece = buf_ref[sl, pl.ds(r, n, stride=dil), :]
            out_ref[r, :, col + sl * LANES:col + (sl + 1) * LANES] = piece.astype(out_ref.dtype)


def _inproj_ab_kernel(h_ref, g_ref, w_ref, cos_ref, sin_ref, qa_ref, ka_ref, va_ref, b0_ref, b1_ref,
                      b2_ref, buf_ref, *, tn):
    xn = _rms(h_ref[...], g_ref[...]).astype(BF16)
    cos = cos_ref[...]
    sin = sin_ref[...]
    scale = HEAD_DIM ** -0.5
    b_base = _A_Q + 2 * _A_KV
    b_refs = (b0_ref, b1_ref, b2_ref)
    for c in range(w_ref.shape[1] // tn):
        col = c * tn
        y = jnp.dot(xn, w_ref[:, col:col + tn], preferred_element_type=F32)
        if col < b_base:
            is_q, is_v = col < _A_Q, col >= _A_Q + _A_KV
        else:
            part = ((col - b_base) % (3 * _B_W)) // _B_W
            is_q, is_v = part == 0, part == 2
        if not is_v:
            y = jnp.concatenate(
                [_rope(y[:, j:j + LANES], cos, sin) for j in range(0, tn, LANES)], axis=1)
        if is_q:
            y = y * scale
        if col < _A_Q:
            qa_ref[:, col:col + tn] = y.astype(BF16)
        elif col < _A_Q + _A_KV:
            ka_ref[:, col - _A_Q:col - _A_Q + tn] = y.astype(BF16)
        elif col < b_base:
            va_ref[:, col - _A_Q - _A_KV:col - _A_Q - _A_KV + tn] = y.astype(BF16)
        else:
            gi, off = divmod(col - b_base, 3 * _B_W)
            dil = B_PAIRS[gi][1]
            if dil == 1:
                b_refs[gi][:, off:off + tn] = y.astype(BF16)
            else:
                _scatter_residues(y, buf_ref, b_refs[gi], dil, off)


def _inproj_ab(h, gain, w_in, cos, sin, layer, j, bsz, *, tm=512, tn=256):
    t, d = h.shape
    s = t // bsz
    nts = s // tm
    n = w_in.shape[-1]
    row = lambda i: (i, 0)
    tab = lambda i: (i % nts, 0)
    res = lambda i: (i // nts, 0, i % nts, 0)
    out_specs = [pl.BlockSpec((tm, _A_Q), row), pl.BlockSpec((tm, _A_KV), row),
                 pl.BlockSpec((tm, _A_KV), row), pl.BlockSpec((tm, 3 * _B_W), row)]
    out_shape = [jax.ShapeDtypeStruct((t, _A_Q), BF16), jax.ShapeDtypeStruct((t, _A_KV), BF16),
                 jax.ShapeDtypeStruct((t, _A_KV), BF16), jax.ShapeDtypeStruct((t, 3 * _B_W), BF16)]
    for _, dil in B_PAIRS[1:]:
        out_specs.append(pl.BlockSpec((None, dil, tm // dil, 3 * _B_W), res))
        out_shape.append(jax.ShapeDtypeStruct((bsz, dil, s // dil, 3 * _B_W), BF16))
    return pl.pallas_call(
        functools.partial(_inproj_ab_kernel, tn=tn),
        grid=(t // tm,),
        in_specs=[
            pl.BlockSpec((tm, d), row),
            _resident((None, 1, d), lambda i: (layer, 0, 0)),
            _resident((None, d, n), lambda i: (j, 0, 0)),
            pl.BlockSpec((tm, LANES), tab),
            pl.BlockSpec((tm, LANES), tab),
        ],
        out_specs=out_specs,
        out_shape=out_shape,
        scratch_shapes=[pltpu.VMEM((tn // LANES, tm, LANES), F32)],
        compiler_params=_params(1),
        name="inproj_ab",
    )(h, gain, w_in, cos, sin)


def _band_kernel(*refs, radius, qpk, length, use_sink, want_lse, tq, n_sub):
    refs = list(refs)
    sink_ref = refs.pop(0) if use_sink else None
    q_ref, k_ref, v_ref, o_ref = refs[:4]
    lse_ref = refs[4] if want_lse else None
    win = tq + 2 * radius
    n_kg = k_ref.shape[1] // LANES
    nb = 2 * qpk
    t = pl.program_id(2)
    left = lax.broadcasted_iota(jnp.int32, (tq, LANES), 1) < HEAD_DIM
    zero = jnp.zeros((tq, LANES), BF16)
    rel0 = (lax.broadcasted_iota(jnp.int32, (tq, win), 1)
            - lax.broadcasted_iota(jnp.int32, (tq, win), 0))

    def body(jj, carry):
        r0 = pl.multiple_of(jj * tq, tq)
        rows = pl.ds(r0, tq)
        qpos0 = (t * n_sub + jj) * tq
        kstart = pl.multiple_of(jnp.clip(qpos0 - radius, 0, length - win), radius)
        bias = jnp.where(jnp.abs(rel0 + (kstart - qpos0)) <= radius, 0.0, NEG_INF)
        for kg in range(n_kg):
            lanes = slice(kg * LANES, (kg + 1) * LANES)
            kwin = k_ref[pl.ds(kstart, win), lanes]
            vwin = v_ref[pl.ds(kstart, win), lanes]
            blocks = []
            for i in range(qpk):
                p = kg * qpk + i
                qp = q_ref[rows, p * LANES:(p + 1) * LANES]
                blocks += [jnp.where(left, qp, zero), jnp.where(left, zero, qp)]
            s = lax.dot_general(jnp.concatenate(blocks, axis=0), kwin, (((1,), (1,)), ((), ())),
                                preferred_element_type=F32)
            es, dens, ms = [], [], []
            for b in range(nb):
                sb = s[b * tq:(b + 1) * tq] + bias
                mb = jnp.max(sb, axis=-1, keepdims=True)
                if use_sink:
                    sink = sink_ref[kg * nb + b]
                    mb = jnp.maximum(mb, sink)
                eb = jnp.exp(sb - mb)
                db = jnp.sum(eb, axis=-1, keepdims=True)
                if use_sink:
                    db = db + jnp.exp(sink - mb)
                es.append(eb.astype(BF16))
                dens.append(db)
                ms.append(mb)
            o2 = jnp.dot(jnp.concatenate(es, axis=0), vwin, preferred_element_type=F32)
            for i in range(qpk):
                p = kg * qpk + i
                even = o2[(2 * i) * tq:(2 * i + 1) * tq] / dens[2 * i]
                odd = o2[(2 * i + 1) * tq:(2 * i + 2) * tq] / dens[2 * i + 1]
                o_ref[rows, p * LANES:(p + 1) * LANES] = jnp.where(left, even, odd).astype(o_ref.dtype)
                if want_lse:
                    lse_e = ms[2 * i] + jnp.log(dens[2 * i])
                    lse_o = ms[2 * i + 1] + jnp.log(dens[2 * i + 1])
                    lse_ref[rows, p * LANES:(p + 1) * LANES] = jnp.where(
                        left, jnp.broadcast_to(lse_e, (tq, LANES)), jnp.broadcast_to(lse_o, (tq, LANES)))
        return carry

    lax.fori_loop(0, n_sub, body, 0)


def _banded_attention(q, k, v, *, q_col, k_col, v_col, radius, qpk, sink=None, want_lse=False,
                      name):
    bsz, dil, length, _ = q.shape
    tq = ATT_BLOCK
    kw = 2 * LANES
    qw = kw * qpk
    tile = min(length, 1024)
    n_sub = tile // tq
    in_specs = [
        pl.BlockSpec((None, None, tile, qw), lambda b, r, t: (b, r, t, q_col)),
        pl.BlockSpec((None, None, length, kw), lambda b, r, t: (b, r, 0, k_col)),
        pl.BlockSpec((None, None, length, kw), lambda b, r, t: (b, r, 0, v_col)),
    ]
    args = [q, k, v]
    if sink is not None:
        in_specs.insert(0, pl.BlockSpec(memory_space=pltpu.SMEM))
        args.insert(0, sink)
    out_map = lambda b, r, t: (b, r, t, 0)
    out_specs = [pl.BlockSpec((None, None, tile, qw), out_map)]
    out_shape = [jax.ShapeDtypeStruct((bsz, dil, length, qw), BF16)]
    if want_lse:
        out_specs.append(pl.BlockSpec((None, None, tile, qw), out_map))
        out_shape.append(jax.ShapeDtypeStruct((bsz, dil, length, qw), F32))
    return pl.pallas_call(
        functools.partial(_band_kernel, radius=radius, qpk=qpk, length=length,
                          use_sink=sink is not None, want_lse=want_lse, tq=tq, n_sub=n_sub),
        grid=(bsz, dil, length // tile),
        in_specs=in_specs,
        out_specs=out_specs,
        out_shape=out_shape,
        compiler_params=_params(3),
        name=name,
    )(*args)


def _interleave_residues(src_ref, buf_ref, dil):
    n = src_ref.shape[1]
    slabs = src_ref.shape[2] // LANES
    for r in range(dil):
        x = src_ref[r].astype(F32)
        for sl in range(slabs):
            buf_ref[sl, pl.ds(r, n, stride=dil), :] = x[:, sl * LANES:(sl + 1) * LANES]
    return [buf_ref[sl] for sl in range(slabs)]


def _outproj_ab_kernel(h_ref, oa_ref, o0_ref, l0_ref, o1_ref, l1_ref, o2_ref, l2_ref, w_ref, out_ref,
                       buf_ref):
    dils = [dil for _, dil in B_PAIRS]
    o1 = _interleave_residues(o1_ref, buf_ref.at[0], dils[1])
    l1 = _interleave_residues(l1_ref, buf_ref.at[1], dils[1])
    o2 = _interleave_residues(o2_ref, buf_ref.at[2], dils[2])
    l2 = _interleave_residues(l2_ref, buf_ref.at[3], dils[2])
    merged = []
    for sl in range(len(o1)):
        lanes = slice(sl * LANES, (sl + 1) * LANES)
        l0 = l0_ref[:, lanes]
        m = jnp.maximum(jnp.maximum(l0, l1[sl]), l2[sl])
        e0, e1, e2 = jnp.exp(l0 - m), jnp.exp(l1[sl] - m), jnp.exp(l2[sl] - m)
        ob = (e0 * o0_ref[:, lanes].astype(F32) + e1 * o1[sl] + e2 * o2[sl]) / (e0 + e1 + e2)
        merged.append(ob.astype(BF16))
    na = oa_ref.shape[1]
    y = jnp.dot(oa_ref[...], w_ref[:na, :], preferred_element_type=F32)
    y = y + jnp.dot(jnp.concatenate(merged, axis=1), w_ref[na:, :], preferred_element_type=F32)
    out_ref[...] = h_ref[...] + y


def _outproj_ab(h, oa, obs, lses, w_out, j, bsz, *, tm=512):
    t, d = h.shape
    s = t // bsz
    nts = s // tm
    row = lambda i: (i, 0)
    res = lambda i: (i // nts, 0, i % nts, 0)
    nb = obs[0].shape[-1]
    in_specs = [pl.BlockSpec((tm, d), row), pl.BlockSpec((tm, oa.shape[1]), row),
                pl.BlockSpec((tm, nb), row), pl.BlockSpec((tm, nb), row)]
    args = [h, oa, obs[0], lses[0]]
    for (_, dil), o, lse in zip(B_PAIRS[1:], obs[1:], lses[1:]):
        in_specs += [pl.BlockSpec((None, dil, tm // dil, nb), res)] * 2
        args += [o, lse]
    in_specs.append(_resident((None,) + w_out.shape[1:], lambda i: (j, 0, 0)))
    return pl.pallas_call(
        _outproj_ab_kernel,
        grid=(t // tm,),
        in_specs=in_specs,
        out_specs=pl.BlockSpec((tm, d), row),
        out_shape=jax.ShapeDtypeStruct((t, d), F32),
        scratch_shapes=[pltpu.VMEM((4, nb // LANES, tm, LANES), F32)],
        compiler_params=_params(1),
        name="outproj_ab",
    )(*args, w_out)


def _inproj_c_kernel(h_ref, g_ref, w_ref, q_ref, k_ref, v_ref, *, tn):
    xn = _rms(h_ref[...], g_ref[...]).astype(BF16)
    width = q_ref.shape[1]
    scale = HEAD_DIM ** -0.5
    for c in range(w_ref.shape[1] // tn):
        col = c * tn
        y = jnp.dot(xn, w_ref[:, col:col + tn], preferred_element_type=F32)
        if col < width:
            q_ref[:, col:col + tn] = (y * scale).astype(BF16)
        elif col < 2 * width:
            k_ref[:, col - width:col - width + tn] = y.astype(BF16)
        else:
            v_ref[:, col - 2 * width:col - 2 * width + tn] = y.astype(BF16)


def _inproj_c(h, gain, w_in, layer, j, *, tm=512, tn=256):
    t, d = h.shape
    n = w_in.shape[-1]
    row = lambda i: (i, 0)
    return pl.pallas_call(
        functools.partial(_inproj_c_kernel, tn=tn),
        grid=(t // tm,),
        in_specs=[
            pl.BlockSpec((tm, d), row),
            _resident((None, 1, d), lambda i: (layer, 0, 0)),
            _resident((None, d, n), lambda i: (j, 0, 0)),
        ],
        out_specs=[pl.BlockSpec((tm, n // 3), row)] * 3,
        out_shape=[jax.ShapeDtypeStruct((t, n // 3), BF16)] * 3,
        compiler_params=_params(1),
        name="inproj_c",
    )(h, gain, w_in)


def _na_bias_kernel(rpb_ref, o_ref):
    h = pl.program_id(0)
    n_dr, n_dc = rpb_ref.shape[1], rpb_ref.shape[2]
    j = lax.broadcasted_iota(jnp.int32, (GRID_W, GRID_W), 0)
    c = lax.broadcasted_iota(jnp.int32, (GRID_W, GRID_W), 1)
    start = jnp.clip(j - NA_KW // 2, 0, GRID_W - NA_KW)
    inside = (c >= start) & (c < start + NA_KW)
    dc = c - j + (NA_KW - 1)
    for dr in range(n_dr):
        acc = jnp.full((GRID_W, GRID_W), NEG_INF, F32)
        for t in range(n_dc):
            acc = jnp.where(dc == t, rpb_ref[h, dr, t], acc)
        o_ref[dr] = jnp.where(inside, acc, NEG_INF)


def _na_bias(rpb):
    nh, n_dr, _ = rpb.shape
    return pl.pallas_call(
        _na_bias_kernel,
        grid=(nh,),
        in_specs=[pl.BlockSpec(memory_space=pltpu.SMEM)],
        out_specs=pl.BlockSpec((None, n_dr, GRID_W, GRID_W), lambda h: (h, 0, 0, 0)),
        out_shape=jax.ShapeDtypeStruct((nh, n_dr, GRID_W, GRID_W), F32),
        compiler_params=_params(1),
        name="na_bias",
    )(rpb)


def _na_kernel(q_ref, k_ref, v_ref, tb_ref, o_ref, *, rows_per_step, rows, kh):
    ig = pl.program_id(2)
    left = lax.broadcasted_iota(jnp.int32, (GRID_W, LANES), 1) < HEAD_DIM
    zero = jnp.zeros((GRID_W, LANES), BF16)

    def body(g, carry):
        i = ig * rows_per_step + g
        rs = jnp.clip(i - kh // 2, 0, rows - kh)
        off = rs - i + (NA_KH - 1)
        start = pl.multiple_of(rs * GRID_W, GRID_W)
        kwin = k_ref[pl.ds(start, kh * GRID_W), :]
        vwin = v_ref[pl.ds(start, kh * GRID_W), :]
        qrow = pl.ds(pl.multiple_of(g * GRID_W, GRID_W), GRID_W)
        qi = q_ref[qrow, :]
        q2 = jnp.concatenate([jnp.where(left, qi, zero), jnp.where(left, zero, qi)], axis=0)
        s = lax.dot_general(q2, kwin, (((1,), (1,)), ((), ())), preferred_element_type=F32)
        s = s + tb_ref[off]
        m = jnp.max(s, axis=-1, keepdims=True)
        e = jnp.exp(s - m)
        den = jnp.sum(e, axis=-1, keepdims=True)
        o2 = jnp.dot(e.astype(BF16), vwin, preferred_element_type=F32) / den
        o_ref[qrow, :] = jnp.where(left, o2[:GRID_W], o2[GRID_W:]).astype(o_ref.dtype)
        return carry

    lax.fori_loop(0, rows_per_step, body, 0, unroll=True)


def _neighbourhood_attention(q, k, v, bias, *, rows_per_step=8):
    bsz, s, width = q.shape
    rows = s // GRID_W
    kh = min(NA_KH, rows)
    n_pairs = width // LANES
    g = rows_per_step
    return pl.pallas_call(
        functools.partial(_na_kernel, rows_per_step=g, rows=rows, kh=kh),
        grid=(bsz, n_pairs, rows // g),
        in_specs=[
            pl.BlockSpec((None, g * GRID_W, LANES), lambda b, p, i: (b, i, p)),
            pl.BlockSpec((None, s, LANES), lambda b, p, i: (b, 0, p)),
            pl.BlockSpec((None, s, LANES), lambda b, p, i: (b, 0, p)),
            pl.BlockSpec((bias.shape[0], None) + bias.shape[2:], lambda b, p, i: (0, p, 0, 0)),
        ],
        out_specs=pl.BlockSpec((None, g * GRID_W, LANES), lambda b, p, i: (b, i, p)),
        out_shape=jax.ShapeDtypeStruct((bsz, s, width), BF16),
        compiler_params=_params(3),
        name="na_attention",
    )(q, k, v, bias)


def _outproj_c_kernel(h_ref, o_ref, w_ref, out_ref):
    out_ref[...] = h_ref[...] + jnp.dot(o_ref[...], w_ref[...], preferred_element_type=F32)


def _outproj_c(h, o, w_out, j, *, tm=512):
    t, d = h.shape
    row = lambda i: (i, 0)
    return pl.pallas_call(
        _outproj_c_kernel,
        grid=(t // tm,),
        in_specs=[pl.BlockSpec((tm, d), row), pl.BlockSpec((tm, o.shape[1]), row),
                  _resident((None,) + w_out.shape[1:], lambda i: (j, 0, 0))],
        out_specs=pl.BlockSpec((tm, d), row),
        out_shape=jax.ShapeDtypeStruct((t, d), F32),
        compiler_params=_params(1),
        name="outproj_c",
    )(h, o, w_out)


def _ple_kernel(h_ref, g_ref, wg_ref, p_ref, wp_ref, gf_ref, o_ref, *, final):
    x = h_ref[...]
    xn = _rms(x, g_ref[...]).astype(BF16)
    gate = jax.nn.sigmoid(jnp.dot(xn, wg_ref[...], preferred_element_type=F32))
    proj = jnp.dot(p_ref[...].astype(BF16), wp_ref[...], preferred_element_type=F32)
    y = x + gate * proj
    o_ref[...] = _rms(y, gf_ref[...]) if final else y


def _ple(h, gain, w_gate, p, w_proj, gain_final, layer, *, final, tm=512):
    t, d = h.shape
    pd = p.shape[-1]
    row = lambda i: (i, 0)
    return pl.pallas_call(
        functools.partial(_ple_kernel, final=final),
        grid=(t // tm,),
        in_specs=[
            pl.BlockSpec((tm, d), row),
            _resident((None, 1, d), lambda i: (layer, 0, 0)),
            _resident((None, d, d), lambda i: (layer, 0, 0)),
            pl.BlockSpec((None, tm, pd), lambda i: (layer, i, 0)),
            _resident((None, pd, d), lambda i: (layer, 0, 0)),
            _resident((1, d), lambda i: (0, 0)),
        ],
        out_specs=pl.BlockSpec((tm, d), row),
        out_shape=jax.ShapeDtypeStruct((t, d), F32),
        compiler_params=_params(1),
        name="ple",
    )(h, gain, w_gate, p, w_proj, gain_final)


def _rope_tables(s):
    half = HEAD_DIM // 2
    inv = ROPE_THETA ** (-jnp.arange(half, dtype=F32) / half)
    ang = jnp.arange(s, dtype=jnp.int32).astype(F32)[:, None] * inv[None, :]
    cos, sin = jnp.cos(ang), jnp.sin(ang)
    reps = LANES // HEAD_DIM
    return (jnp.tile(jnp.concatenate([cos, cos], axis=1), (1, reps)),
            jnp.tile(jnp.concatenate([-sin, sin], axis=1), (1, reps)))


def _permute_ab_params(w_in, w_out, sink):
    heads = lambda w, axis: [lax.slice_in_dim(w, h * HEAD_DIM, (h + 1) * HEAD_DIM, axis=axis)
                             for h in A_HEAD_ORDER]
    b_base = _A_Q + 2 * _A_KV
    n_groups = len(B_PAIRS)
    groups = [w_in[..., b_base + (part * n_groups + gi) * _B_W:b_base + (part * n_groups + gi + 1) * _B_W]
              for gi in range(n_groups) for part in range(3)]
    w_in = jnp.concatenate(heads(w_in, 2) + [w_in[..., _A_Q:b_base]] + groups, axis=-1)
    w_out = jnp.concatenate(heads(w_out, 1) + [w_out[:, _A_Q:, :]], axis=1)
    return w_in, w_out, sink[:, jnp.array(A_HEAD_ORDER)]


def _ab_mixer(h, gain, w_in, sink, w_out, cos, sin, layer, j, bsz, s):
    qa, ka, va, b0, b1, b2 = _inproj_ab(h, gain, w_in, cos, sin, layer, j, bsz)
    seq = lambda a: a.reshape(bsz, 1, s, a.shape[-1])
    oa = _banded_attention(seq(qa), seq(ka), seq(va), q_col=0, k_col=0, v_col=0, radius=A_RADIUS,
                           qpk=A_GROUP, sink=sink[j], name="band_a")[0]
    obs, lses = [], []
    for gi, (qkv, (window, dil)) in enumerate(zip((seq(b0), b1, b2), B_PAIRS)):
        o, lse = _banded_attention(qkv, qkv, qkv, q_col=0, k_col=1, v_col=2,
                                   radius=window // (2 * dil), qpk=1, want_lse=True,
                                   name=f"band_b{gi}")
        obs.append(o)
        lses.append(lse)
    flat = lambda a: a.reshape(bsz * s, a.shape[-1])
    obs[0], lses[0] = flat(obs[0]), flat(lses[0])
    return _outproj_ab(h, flat(oa), obs, lses, w_out, j, bsz)


def _c_mixer(h, gain, w_in, rpb, w_out, layer, j, bsz, s):
    q, k, v = _inproj_c(h, gain, w_in, layer, j)
    toep = _na_bias(rpb[j])
    nh = toep.shape[0]
    kh = min(NA_KH, s // GRID_W)
    bias = jnp.stack([
        jnp.concatenate([toep[:, off + a] for a in range(kh)], axis=-1) for off in range(NA_KH)])
    bias = bias.reshape(NA_KH, nh // 2, 2 * GRID_W, kh * GRID_W)
    o = _neighbourhood_attention(q.reshape(bsz, s, -1), k.reshape(bsz, s, -1),
                                 v.reshape(bsz, s, -1), bias)
    return _outproj_c(h, o.reshape(bsz * s, -1), w_out, j)


def kernel(x, p, norm_ffn1, ffn1_w_gate, ffn1_w_up, ffn1_w_down, norm_mix, w_in_ab, sink_a, w_out_ab, w_in_c, rpb_c, w_out_c, norm_ffn2, ffn2_w_gate, ffn2_w_up, ffn2_w_down, norm_ple, w_ple_gate, w_ple_proj, norm_final):
    bsz, s, d = x.shape
    depth = p.shape[0]
    bf = lambda w: w.astype(BF16)
    gains = lambda g: g.reshape(g.shape[0], 1, d)
    ffn1 = (gains(norm_ffn1), bf(ffn1_w_gate), bf(ffn1_w_up), bf(ffn1_w_down))
    ffn2 = (gains(norm_ffn2), bf(ffn2_w_gate), bf(ffn2_w_up), bf(ffn2_w_down))
    g_mix, g_ple = gains(norm_mix), gains(norm_ple)
    w_in_ab, w_out_ab, w_in_c, w_out_c = bf(w_in_ab), bf(w_out_ab), bf(w_in_c), bf(w_out_c)
    w_in_ab, w_out_ab, sink_a = _permute_ab_params(w_in_ab, w_out_ab, sink_a)
    w_ple_gate, w_ple_proj = bf(w_ple_gate), bf(w_ple_proj)
    p2 = p.reshape(depth, bsz * s, p.shape[-1])
    g_final = norm_final.reshape(1, d)
    cos, sin = _rope_tables(s)

    h = x.reshape(bsz * s, d)
    for i in range(depth):
        h = _ffn(h, *ffn1, i)
        j = i // 2
        if i % 2 == 0:
            h = _ab_mixer(h, g_mix, w_in_ab, sink_a, w_out_ab, cos, sin, i, j, bsz, s)
        else:
            h = _c_mixer(h, g_mix, w_in_c, rpb_c, w_out_c, i, j, bsz, s)
        h = _ffn(h, *ffn2, i)
        h = _ple(h, g_ple, w_ple_gate, p2, w_ple_proj, g_final, i, final=i == depth - 1)
    return h.reshape(bsz, s, d)
```

```python
import functools

import jax
import jax.numpy as jnp
from jax import lax
from jax.experimental import pallas as pl
from jax.experimental.pallas import tpu as pltpu

HEAD_DIM = 64
GRID_W = 64
ROPE_THETA = 10000.0
RMS_EPS = 1e-6
A_HEADS = 12
A_KV_HEADS = 4
A_RADIUS = 128
B_PAIRS = ((128, 1), (512, 4), (2048, 16))
B_SLOTS = 4
NA_KH = 8
NA_KW = 16
NEG_INF = -1e30

LANES = 128
ATT_BLOCK = 128
VMEM_LIMIT = 56 * 1024 * 1024
F32 = jnp.float32
BF16 = jnp.bfloat16


def _params(n_axes):
    return pltpu.CompilerParams(dimension_semantics=("arbitrary",) * n_axes,
                                vmem_limit_bytes=VMEM_LIMIT)


def _resident(block_shape, index_map):
    return pl.BlockSpec(block_shape, index_map, pipeline_mode=pl.Buffered(1))


def _rms(x, g):
    ms = jnp.mean(x * x, axis=-1, keepdims=True)
    return x * lax.rsqrt(ms + RMS_EPS) * g


def _ffn_kernel(h_ref, g_ref, wg_ref, wu_ref, wd_ref, o_ref, a_ref, *, tf):
    x = h_ref[...]
    xn = _rms(x, g_ref[...]).astype(BF16)
    for f in range(wg_ref.shape[1] // tf):
        sl = slice(f * tf, (f + 1) * tf)
        gate = jnp.dot(xn, wg_ref[:, sl], preferred_element_type=F32)
        up = jnp.dot(xn, wu_ref[:, sl], preferred_element_type=F32)
        a_ref[:, sl] = (gate * jax.nn.sigmoid(gate) * up).astype(BF16)
    y = jnp.dot(a_ref[...], wd_ref[...], preferred_element_type=F32)
    o_ref[...] = x + 0.5 * y


def _ffn(h, gain, w_gate, w_up, w_down, layer, *, tm=512, tf=256):
    t, d = h.shape
    f = w_gate.shape[-1]
    row = lambda i: (i, 0)
    return pl.pallas_call(
        functools.partial(_ffn_kernel, tf=tf),
        grid=(t // tm,),
        in_specs=[
            pl.BlockSpec((tm, d), row),
            _resident((None, 1, d), lambda i: (layer, 0, 0)),
            _resident((None, d, f), lambda i: (layer, 0, 0)),
            _resident((None, d, f), lambda i: (layer, 0, 0)),
            _resident((None, f, d), lambda i: (layer, 0, 0)),
        ],
        out_specs=pl.BlockSpec((tm, d), row),
        out_shape=jax.ShapeDtypeStruct((t, d), F32),
        scratch_shapes=[pltpu.VMEM((tm, f), BF16)],
        compiler_params=_params(1),
        name="ffn",
    )(h, gain, w_gate, w_up, w_down)


def _rope(y, cos, sin_signed):
    lane = lax.broadcasted_iota(jnp.int32, y.shape, 1)
    first_half = (lane & (HEAD_DIM // 2)) == 0
    swapped = jnp.where(first_half, pltpu.roll(y, LANES - HEAD_DIM // 2, 1),
                        pltpu.roll(y, HEAD_DIM // 2, 1))
    return y * cos + swapped * sin_signed


A_GROUP = A_HEADS // A_KV_HEADS
A_HEAD_ORDER = (0, 3, 1, 4, 2, 5, 6, 9, 7, 10, 8, 11)
_A_Q, _A_KV = A_HEADS * HEAD_DIM, A_KV_HEADS * HEAD_DIM
_B_W = B_SLOTS * HEAD_DIM


def _scatter_residues(y, buf_ref, out_ref, dil, col):
    n = y.shape[0] // dil
    slabs = y.shape[1] // LANES
    for sl in range(slabs):
        buf_ref[sl] = y[:, sl * LANES:(sl + 1) * LANES]
    for r in range(dil):
        for sl in range(slabs):
            piece = buf_ref[sl, pl.ds(r, n, stride=dil), :]
            out_ref[r, :, col + sl * LANES:col + (sl + 1) * LANES] = piece.astype(out_ref.dtype)


def _inproj_ab_kernel(h_ref, g_ref, w_ref, cos_ref, sin_ref, qa_ref, ka_ref, va_ref, b0_ref, b1_ref,
                      b2_ref, buf_ref, *, tn):
    xn = _rms(h_ref[...], g_ref[...]).astype(BF16)
    cos = cos_ref[...]
    sin = sin_ref[...]
    scale = HEAD_DIM ** -0.5
    b_base = _A_Q + 2 * _A_KV
    b_refs = (b0_ref, b1_ref, b2_ref)
    for c in range(w_ref.shape[1] // tn):
        col = c * tn
        y = jnp.dot(xn, w_ref[:, col:col + tn], preferred_element_type=F32)
        if col < b_base:
            is_q, is_v = col < _A_Q, col >= _A_Q + _A_KV
        else:
            part = ((col - b_base) % (3 * _B_W)) // _B_W
            is_q, is_v = part == 0, part == 2
        if not is_v:
            y = jnp.concatenate(
                [_rope(y[:, j:j + LANES], cos, sin) for j in range(0, tn, LANES)], axis=1)
        if is_q:
            y = y * scale
        if col < _A_Q:
            qa_ref[:, col:col + tn] = y.astype(BF16)
        elif col < _A_Q + _A_KV:
            ka_ref[:, col - _A_Q:col - _A_Q + tn] = y.astype(BF16)
        elif col < b_base:
            va_ref[:, col - _A_Q - _A_KV:col - _A_Q - _A_KV + tn] = y.astype(BF16)
        else:
            gi, off = divmod(col - b_base, 3 * _B_W)
            dil = B_PAIRS[gi][1]
            if dil == 1:
                b_refs[gi][:, off:off + tn] = y.astype(BF16)
            else:
                _scatter_residues(y, buf_ref, b_refs[gi], dil, off)


def _inproj_ab(h, gain, w_in, cos, sin, layer, j, bsz, *, tm=512, tn=256):
    t, d = h.shape
    s = t // bsz
    nts = s // tm
    n = w_in.shape[-1]
    row = lambda i: (i, 0)
    tab = lambda i: (i % nts, 0)
    res = lambda i: (i // nts, 0, i % nts, 0)
    out_specs = [pl.BlockSpec((tm, _A_Q), row), pl.BlockSpec((tm, _A_KV), row),
                 pl.BlockSpec((tm, _A_KV), row), pl.BlockSpec((tm, 3 * _B_W), row)]
    out_shape = [jax.ShapeDtypeStruct((t, _A_Q), BF16), jax.ShapeDtypeStruct((t, _A_KV), BF16),
                 jax.ShapeDtypeStruct((t, _A_KV), BF16), jax.ShapeDtypeStruct((t, 3 * _B_W), BF16)]
    for _, dil in B_PAIRS[1:]:
        out_specs.append(pl.BlockSpec((None, dil, tm // dil, 3 * _B_W), res))
        out_shape.append(jax.ShapeDtypeStruct((bsz, dil, s // dil, 3 * _B_W), BF16))
    return pl.pallas_call(
        functools.partial(_inproj_ab_kernel, tn=tn),
        grid=(t // tm,),
        in_specs=[
            pl.BlockSpec((tm, d), row),
            _resident((None, 1, d), lambda i: (layer, 0, 0)),
            _resident((None, d, n), lambda i: (j, 0, 0)),
            pl.BlockSpec((tm, LANES), tab),
            pl.BlockSpec((tm, LANES), tab),
        ],
        out_specs=out_specs,
        out_shape=out_shape,
        scratch_shapes=[pltpu.VMEM((tn // LANES, tm, LANES), F32)],
        compiler_params=_params(1),
        name="inproj_ab",
    )(h, gain, w_in, cos, sin)


def _band_kernel(*refs, radius, qpk, length, use_sink, want_lse, tq, n_sub, unroll):
    refs = list(refs)
    sink_ref = refs.pop(0) if use_sink else None
    q_ref, k_ref, v_ref, o_ref = refs[:4]
    lse_ref = refs[4] if want_lse else None
    win = tq + 2 * radius
    n_kg = k_ref.shape[1] // LANES
    nb = 2 * qpk
    t = pl.program_id(2)
    left = lax.broadcasted_iota(jnp.int32, (tq, LANES), 1) < HEAD_DIM
    zero = jnp.zeros((tq, LANES), BF16)
    rel0 = (lax.broadcasted_iota(jnp.int32, (tq, win), 1)
            - lax.broadcasted_iota(jnp.int32, (tq, win), 0))

    def body(jj, carry):
        jobs = []
        for u in range(unroll):
            sub = jj * unroll + u
            rows = pl.ds(pl.multiple_of(sub * tq, tq), tq)
            qpos0 = (t * n_sub + sub) * tq
            kstart = pl.multiple_of(jnp.clip(qpos0 - radius, 0, length - win), radius)
            bias = jnp.where(jnp.abs(rel0 + (kstart - qpos0)) <= radius, 0.0, NEG_INF)
            jobs += [(rows, pl.ds(kstart, win), bias, kg) for kg in range(n_kg)]
        scores = []
        for rows, keys, bias, kg in jobs:
            blocks = []
            for i in range(qpk):
                p = kg * qpk + i
                qp = q_ref[rows, p * LANES:(p + 1) * LANES]
                blocks += [jnp.where(left, qp, zero), jnp.where(left, zero, qp)]
            scores.append(lax.dot_general(
                jnp.concatenate(blocks, axis=0), k_ref[keys, kg * LANES:(kg + 1) * LANES],
                (((1,), (1,)), ((), ())), preferred_element_type=F32))
        stats = []
        for (rows, keys, bias, kg), s in zip(jobs, scores):
            es, dens, ms = [], [], []
            for b in range(nb):
                sb = s[b * tq:(b + 1) * tq] + bias
                mb = jnp.max(sb, axis=-1, keepdims=True)
                if use_sink:
                    sink = sink_ref[kg * nb + b]
                    mb = jnp.maximum(mb, sink)
                eb = jnp.exp(sb - mb)
                db = jnp.sum(eb, axis=-1, keepdims=True)
                if use_sink:
                    db = db + jnp.exp(sink - mb)
                es.append(eb.astype(BF16))
                dens.append(db)
                ms.append(mb)
            stats.append((jnp.concatenate(es, axis=0), dens, ms))
        for (rows, keys, bias, kg), (e, dens, ms) in zip(jobs, stats):
            o2 = jnp.dot(e, v_ref[keys, kg * LANES:(kg + 1) * LANES], preferred_element_type=F32)
            for i in range(qpk):
                p = kg * qpk + i
                even = o2[(2 * i) * tq:(2 * i + 1) * tq] / dens[2 * i]
                odd = o2[(2 * i + 1) * tq:(2 * i + 2) * tq] / dens[2 * i + 1]
                o_ref[rows, p * LANES:(p + 1) * LANES] = jnp.where(left, even, odd).astype(o_ref.dtype)
                if want_lse:
                    lse_e = ms[2 * i] + jnp.log(dens[2 * i])
                    lse_o = ms[2 * i + 1] + jnp.log(dens[2 * i + 1])
                    lse_ref[rows, p * LANES:(p + 1) * LANES] = jnp.where(
                        left, jnp.broadcast_to(lse_e, (tq, LANES)), jnp.broadcast_to(lse_o, (tq, LANES)))
        return carry

    lax.fori_loop(0, n_sub // unroll, body, 0)


def _banded_attention(q, k, v, *, q_col, k_col, v_col, radius, qpk, sink=None, want_lse=False,
                      name):
    bsz, dil, length, _ = q.shape
    tq = ATT_BLOCK
    kw = 2 * LANES
    qw = kw * qpk
    tile = min(length, 1024)
    n_sub = tile // tq
    in_specs = [
        pl.BlockSpec((None, None, tile, qw), lambda b, r, t: (b, r, t, q_col)),
        pl.BlockSpec((None, None, length, kw), lambda b, r, t: (b, r, 0, k_col)),
        pl.BlockSpec((None, None, length, kw), lambda b, r, t: (b, r, 0, v_col)),
    ]
    args = [q, k, v]
    if sink is not None:
        in_specs.insert(0, pl.BlockSpec(memory_space=pltpu.SMEM))
        args.insert(0, sink)
    out_map = lambda b, r, t: (b, r, t, 0)
    out_specs = [pl.BlockSpec((None, None, tile, qw), out_map)]
    out_shape = [jax.ShapeDtypeStruct((bsz, dil, length, qw), BF16)]
    if want_lse:
        out_specs.append(pl.BlockSpec((None, None, tile, qw), out_map))
        out_shape.append(jax.ShapeDtypeStruct((bsz, dil, length, qw), F32))
    return pl.pallas_call(
        functools.partial(_band_kernel, radius=radius, qpk=qpk, length=length,
                          use_sink=sink is not None, want_lse=want_lse, tq=tq, n_sub=n_sub,
                          unroll=max(1, min(n_sub, 4 // qpk))),
        grid=(bsz, dil, length // tile),
        in_specs=in_specs,
        out_specs=out_specs,
        out_shape=out_shape,
        compiler_params=_params(3),
        name=name,
    )(*args)


def _interleave_residues(src_ref, buf_ref, dil):
    n = src_ref.shape[1]
    slabs = src_ref.shape[2] // LANES
    for r in range(dil):
        x = src_ref[r].astype(F32)
        for sl in range(slabs):
            buf_ref[sl, pl.ds(r, n, stride=dil), :] = x[:, sl * LANES:(sl + 1) * LANES]
    return [buf_ref[sl] for sl in range(slabs)]


def _outproj_ab_kernel(h_ref, oa_ref, o0_ref, l0_ref, o1_ref, l1_ref, o2_ref, l2_ref, w_ref, out_ref,
                       buf_ref):
    dils = [dil for _, dil in B_PAIRS]
    o1 = _interleave_residues(o1_ref, buf_ref.at[0], dils[1])
    l1 = _interleave_residues(l1_ref, buf_ref.at[1], dils[1])
    o2 = _interleave_residues(o2_ref, buf_ref.at[2], dils[2])
    l2 = _interleave_residues(l2_ref, buf_ref.at[3], dils[2])
    merged = []
    for sl in range(len(o1)):
        lanes = slice(sl * LANES, (sl + 1) * LANES)
        l0 = l0_ref[:, lanes]
        m = jnp.maximum(jnp.maximum(l0, l1[sl]), l2[sl])
        e0, e1, e2 = jnp.exp(l0 - m), jnp.exp(l1[sl] - m), jnp.exp(l2[sl] - m)
        ob = (e0 * o0_ref[:, lanes].astype(F32) + e1 * o1[sl] + e2 * o2[sl]) / (e0 + e1 + e2)
        merged.append(ob.astype(BF16))
    na = oa_ref.shape[1]
    y = jnp.dot(oa_ref[...], w_ref[:na, :], preferred_element_type=F32)
    y = y + jnp.dot(jnp.concatenate(merged, axis=1), w_ref[na:, :], preferred_element_type=F32)
    out_ref[...] = h_ref[...] + y


def _outproj_ab(h, oa, obs, lses, w_out, j, bsz, *, tm=512):
    t, d = h.shape
    s = t // bsz
    nts = s // tm
    row = lambda i: (i, 0)
    res = lambda i: (i // nts, 0, i % nts, 0)
    nb = obs[0].shape[-1]
    in_specs = [pl.BlockSpec((tm, d), row), pl.BlockSpec((tm, oa.shape[1]), row),
                pl.BlockSpec((tm, nb), row), pl.BlockSpec((tm, nb), row)]
    args = [h, oa, obs[0], lses[0]]
    for (_, dil), o, lse in zip(B_PAIRS[1:], obs[1:], lses[1:]):
        in_specs += [pl.BlockSpec((None, dil, tm // dil, nb), res)] * 2
        args += [o, lse]
    in_specs.append(_resident((None,) + w_out.shape[1:], lambda i: (j, 0, 0)))
    return pl.pallas_call(
        _outproj_ab_kernel,
        grid=(t // tm,),
        in_specs=in_specs,
        out_specs=pl.BlockSpec((tm, d), row),
        out_shape=jax.ShapeDtypeStruct((t, d), F32),
        scratch_shapes=[pltpu.VMEM((4, nb // LANES, tm, LANES), F32)],
        compiler_params=_params(1),
        name="outproj_ab",
    )(*args, w_out)


def _inproj_c_kernel(h_ref, g_ref, w_ref, q_ref, k_ref, v_ref, *, tn):
    xn = _rms(h_ref[...], g_ref[...]).astype(BF16)
    width = q_ref.shape[1]
    scale = HEAD_DIM ** -0.5
    for c in range(w_ref.shape[1] // tn):
        col = c * tn
        y = jnp.dot(xn, w_ref[:, col:col + tn], preferred_element_type=F32)
        if col < width:
            q_ref[:, col:col + tn] = (y * scale).astype(BF16)
        elif col < 2 * width:
            k_ref[:, col - width:col - width + tn] = y.astype(BF16)
        else:
            v_ref[:, col - 2 * width:col - 2 * width + tn] = y.astype(BF16)


def _inproj_c(h, gain, w_in, layer, j, *, tm=512, tn=256):
    t, d = h.shape
    n = w_in.shape[-1]
    row = lambda i: (i, 0)
    return pl.pallas_call(
        functools.partial(_inproj_c_kernel, tn=tn),
        grid=(t // tm,),
        in_specs=[
            pl.BlockSpec((tm, d), row),
            _resident((None, 1, d), lambda i: (layer, 0, 0)),
            _resident((None, d, n), lambda i: (j, 0, 0)),
        ],
        out_specs=[pl.BlockSpec((tm, n // 3), row)] * 3,
        out_shape=[jax.ShapeDtypeStruct((t, n // 3), BF16)] * 3,
        compiler_params=_params(1),
        name="inproj_c",
    )(h, gain, w_in)


def _na_bias_kernel(rpb_ref, o_ref):
    h = pl.program_id(0)
    n_dr, n_dc = rpb_ref.shape[1], rpb_ref.shape[2]
    j = lax.broadcasted_iota(jnp.int32, (GRID_W, GRID_W), 0)
    c = lax.broadcasted_iota(jnp.int32, (GRID_W, GRID_W), 1)
    start = jnp.clip(j - NA_KW // 2, 0, GRID_W - NA_KW)
    inside = (c >= start) & (c < start + NA_KW)
    dc = c - j + (NA_KW - 1)
    for dr in range(n_dr):
        acc = jnp.full((GRID_W, GRID_W), NEG_INF, F32)
        for t in range(n_dc):
            acc = jnp.where(dc == t, rpb_ref[h, dr, t], acc)
        o_ref[dr] = jnp.where(inside, acc, NEG_INF)


def _na_bias(rpb):
    nh, n_dr, _ = rpb.shape
    return pl.pallas_call(
        _na_bias_kernel,
        grid=(nh,),
        in_specs=[pl.BlockSpec(memory_space=pltpu.SMEM)],
        out_specs=pl.BlockSpec((None, n_dr, GRID_W, GRID_W), lambda h: (h, 0, 0, 0)),
        out_shape=jax.ShapeDtypeStruct((nh, n_dr, GRID_W, GRID_W), F32),
        compiler_params=_params(1),
        name="na_bias",
    )(rpb)


def _na_kernel(q_ref, k_ref, v_ref, tb_ref, o_ref, *, rows_per_step, rows, kh):
    ig = pl.program_id(2)
    left = lax.broadcasted_iota(jnp.int32, (GRID_W, LANES), 1) < HEAD_DIM
    zero = jnp.zeros((GRID_W, LANES), BF16)

    windows, scores = [], []
    for g in range(rows_per_step):
        i = ig * rows_per_step + g
        rs = jnp.clip(i - kh // 2, 0, rows - kh)
        off = rs - i + (NA_KH - 1)
        win = pl.ds(pl.multiple_of(rs * GRID_W, GRID_W), kh * GRID_W)
        qi = q_ref[g * GRID_W:(g + 1) * GRID_W, :]
        q2 = jnp.concatenate([jnp.where(left, qi, zero), jnp.where(left, zero, qi)], axis=0)
        s = lax.dot_general(q2, k_ref[win, :], (((1,), (1,)), ((), ())), preferred_element_type=F32)
        scores.append(s + tb_ref[off])
        windows.append(win)
    probs, dens = [], []
    for s in scores:
        e = jnp.exp(s - jnp.max(s, axis=-1, keepdims=True))
        dens.append(jnp.sum(e, axis=-1, keepdims=True))
        probs.append(e.astype(BF16))
    for g in range(rows_per_step):
        o2 = jnp.dot(probs[g], v_ref[windows[g], :], preferred_element_type=F32) / dens[g]
        o_ref[g * GRID_W:(g + 1) * GRID_W, :] = jnp.where(
            left, o2[:GRID_W], o2[GRID_W:]).astype(o_ref.dtype)


def _neighbourhood_attention(q, k, v, bias, *, rows_per_step=8):
    bsz, s, width = q.shape
    rows = s // GRID_W
    kh = min(NA_KH, rows)
    n_pairs = width // LANES
    g = rows_per_step
    return pl.pallas_call(
        functools.partial(_na_kernel, rows_per_step=g, rows=rows, kh=kh),
        grid=(bsz, n_pairs, rows // g),
        in_specs=[
            pl.BlockSpec((None, g * GRID_W, LANES), lambda b, p, i: (b, i, p)),
            pl.BlockSpec((None, s, LANES), lambda b, p, i: (b, 0, p)),
            pl.BlockSpec((None, s, LANES), lambda b, p, i: (b, 0, p)),
            pl.BlockSpec((bias.shape[0], None) + bias.shape[2:], lambda b, p, i: (0, p, 0, 0)),
        ],
        out_specs=pl.BlockSpec((None, g * GRID_W, LANES), lambda b, p, i: (b, i, p)),
        out_shape=jax.ShapeDtypeStruct((bsz, s, width), BF16),
        compiler_params=_params(3),
        name="na_attention",
    )(q, k, v, bias)


def _outproj_c_kernel(h_ref, o_ref, w_ref, out_ref):
    out_ref[...] = h_ref[...] + jnp.dot(o_ref[...], w_ref[...], preferred_element_type=F32)


def _outproj_c(h, o, w_out, j, *, tm=512):
    t, d = h.shape
    row = lambda i: (i, 0)
    return pl.pallas_call(
        _outproj_c_kernel,
        grid=(t // tm,),
        in_specs=[pl.BlockSpec((tm, d), row), pl.BlockSpec((tm, o.shape[1]), row),
                  _resident((None,) + w_out.shape[1:], lambda i: (j, 0, 0))],
        out_specs=pl.BlockSpec((tm, d), row),
        out_shape=jax.ShapeDtypeStruct((t, d), F32),
        compiler_params=_params(1),
        name="outproj_c",
    )(h, o, w_out)


def _ple_kernel(h_ref, g_ref, wg_ref, p_ref, wp_ref, gf_ref, o_ref, *, final):
    x = h_ref[...]
    xn = _rms(x, g_ref[...]).astype(BF16)
    gate = jax.nn.sigmoid(jnp.dot(xn, wg_ref[...], preferred_element_type=F32))
    proj = jnp.dot(p_ref[...].astype(BF16), wp_ref[...], preferred_element_type=F32)
    y = x + gate * proj
    o_ref[...] = _rms(y, gf_ref[...]) if final else y


def _ple(h, gain, w_gate, p, w_proj, gain_final, layer, *, final, tm=512):
    t, d = h.shape
    pd = p.shape[-1]
    row = lambda i: (i, 0)
    return pl.pallas_call(
        functools.partial(_ple_kernel, final=final),
        grid=(t // tm,),
        in_specs=[
            pl.BlockSpec((tm, d), row),
            _resident((None, 1, d), lambda i: (layer, 0, 0)),
            _resident((None, d, d), lambda i: (layer, 0, 0)),
            pl.BlockSpec((None, tm, pd), lambda i: (layer, i, 0)),
            _resident((None, pd, d), lambda i: (layer, 0, 0)),
            _resident((1, d), lambda i: (0, 0)),
        ],
        out_specs=pl.BlockSpec((tm, d), row),
        out_shape=jax.ShapeDtypeStruct((t, d), F32),
        compiler_params=_params(1),
        name="ple",
    )(h, gain, w_gate, p, w_proj, gain_final)


def _rope_tables(s):
    half = HEAD_DIM // 2
    inv = ROPE_THETA ** (-jnp.arange(half, dtype=F32) / half)
    ang = jnp.arange(s, dtype=jnp.int32).astype(F32)[:, None] * inv[None, :]
    cos, sin = jnp.cos(ang), jnp.sin(ang)
    reps = LANES // HEAD_DIM
    return (jnp.tile(jnp.concatenate([cos, cos], axis=1), (1, reps)),
            jnp.tile(jnp.concatenate([-sin, sin], axis=1), (1, reps)))


def _permute_ab_params(w_in, w_out, sink):
    heads = lambda w, axis: [lax.slice_in_dim(w, h * HEAD_DIM, (h + 1) * HEAD_DIM, axis=axis)
                             for h in A_HEAD_ORDER]
    b_base = _A_Q + 2 * _A_KV
    n_groups = len(B_PAIRS)
    groups = [w_in[..., b_base + (part * n_groups + gi) * _B_W:b_base + (part * n_groups + gi + 1) * _B_W]
              for gi in range(n_groups) for part in range(3)]
    w_in = jnp.concatenate(heads(w_in, 2) + [w_in[..., _A_Q:b_base]] + groups, axis=-1)
    w_out = jnp.concatenate(heads(w_out, 1) + [w_out[:, _A_Q:, :]], axis=1)
    return w_in, w_out, sink[:, jnp.array(A_HEAD_ORDER)]


def _ab_mixer(h, gain, w_in, sink, w_out, cos, sin, layer, j, bsz, s):
    qa, ka, va, b0, b1, b2 = _inproj_ab(h, gain, w_in, cos, sin, layer, j, bsz)
    seq = lambda a: a.reshape(bsz, 1, s, a.shape[-1])
    oa = _banded_attention(seq(qa), seq(ka), seq(va), q_col=0, k_col=0, v_col=0, radius=A_RADIUS,
                           qpk=A_GROUP, sink=sink[j], name="band_a")[0]
    obs, lses = [], []
    for gi, (qkv, (window, dil)) in enumerate(zip((seq(b0), b1, b2), B_PAIRS)):
        o, lse = _banded_attention(qkv, qkv, qkv, q_col=0, k_col=1, v_col=2,
                                   radius=window // (2 * dil), qpk=1, want_lse=True,
                                   name=f"band_b{gi}")
        obs.append(o)
        lses.append(lse)
    flat = lambda a: a.reshape(bsz * s, a.shape[-1])
    obs[0], lses[0] = flat(obs[0]), flat(lses[0])
    return _outproj_ab(h, flat(oa), obs, lses, w_out, j, bsz)


def _c_mixer(h, gain, w_in, rpb, w_out, layer, j, bsz, s):
    q, k, v = _inproj_c(h, gain, w_in, layer, j)
    toep = _na_bias(rpb[j])
    nh = toep.shape[0]
    kh = min(NA_KH, s // GRID_W)
    bias = jnp.stack([
        jnp.concatenate([toep[:, off + a] for a in range(kh)], axis=-1) for off in range(NA_KH)])
    bias = bias.reshape(NA_KH, nh // 2, 2 * GRID_W, kh * GRID_W)
    o = _neighbourhood_attention(q.reshape(bsz, s, -1), k.reshape(bsz, s, -1),
                                 v.reshape(bsz, s, -1), bias)
    return _outproj_c(h, o.reshape(bsz * s, -1), w_out, j)


def kernel(x, p, norm_ffn1, ffn1_w_gate, ffn1_w_up, ffn1_w_down, norm_mix, w_in_ab, sink_a, w_out_ab, w_in_c, rpb_c, w_out_c, norm_ffn2, ffn2_w_gate, ffn2_w_up, ffn2_w_down, norm_ple, w_ple_gate, w_ple_proj, norm_final):
    bsz, s, d = x.shape
    depth = p.shape[0]
    bf = lambda w: w.astype(BF16)
    gains = lambda g: g.reshape(g.shape[0], 1, d)
    ffn1 = (gains(norm_ffn1), bf(ffn1_w_gate), bf(ffn1_w_up), bf(ffn1_w_down))
    ffn2 = (gains(norm_ffn2), bf(ffn2_w_gate), bf(ffn2_w_up), bf(ffn2_w_down))
    g_mix, g_ple = gains(norm_mix), gains(norm_ple)
    w_in_ab, w_out_ab, w_in_c, w_out_c = bf(w_in_ab), bf(w_out_ab), bf(w_in_c), bf(w_out_c)
    w_in_ab, w_out_ab, sink_a = _permute_ab_params(w_in_ab, w_out_ab, sink_a)
    w_ple_gate, w_ple_proj = bf(w_ple_gate), bf(w_ple_proj)
    p2 = p.reshape(depth, bsz * s, p.shape[-1])
    g_final = norm_final.reshape(1, d)
    cos, sin = _rope_tables(s)

    h = x.reshape(bsz * s, d)
    for i in range(depth):
        h = _ffn(h, *ffn1, i)
        j = i // 2
        if i % 2 == 0:
            h = _ab_mixer(h, g_mix, w_in_ab, sink_a, w_out_ab, cos, sin, i, j, bsz, s)
        else:
            h = _c_mixer(h, g_mix, w_in_c, rpb_c, w_out_c, i, j, bsz, s)
        h = _ffn(h, *ffn2, i)
        h = _ple(h, g_ple, w_ple_gate, p2, w_ple_proj, g_final, i, final=i == depth - 1)
    return h.reshape(bsz, s, d)
```

```python
import functools

import jax
import jax.numpy as jnp
from jax import lax
from jax.experimental import pallas as pl
from jax.experimental.pallas import tpu as pltpu

HEAD_DIM = 64
GRID_W = 64
ROPE_THETA = 10000.0
RMS_EPS = 1e-6
A_HEADS = 12
A_KV_HEADS = 4
A_RADIUS = 128
B_PAIRS = ((128, 1), (512, 4), (2048, 16))
B_SLOTS = 4
NA_KH = 8
NA_KW = 16
NEG_INF = -1e30

LANES = 128
ATT_BLOCK = 128
VMEM_LIMIT = 56 * 1024 * 1024
F32 = jnp.float32
BF16 = jnp.bfloat16


def _params(n_axes):
    return pltpu.CompilerParams(dimension_semantics=("arbitrary",) * n_axes,
                                vmem_limit_bytes=VMEM_LIMIT)


def _resident(block_shape, index_map):
    return pl.BlockSpec(block_shape, index_map, pipeline_mode=pl.Buffered(1))


def _rms(x, g):
    ms = jnp.mean(x * x, axis=-1, keepdims=True)
    return x * lax.rsqrt(ms + RMS_EPS) * g


def _ffn_stage(x, g_ref, wg_ref, wu_ref, wd_ref, a_ref, tf):
    xn = _rms(x, g_ref[...]).astype(BF16)
    for f in range(wg_ref.shape[1] // tf):
        sl = slice(f * tf, (f + 1) * tf)
        gate = jnp.dot(xn, wg_ref[:, sl], preferred_element_type=F32)
        up = jnp.dot(xn, wu_ref[:, sl], preferred_element_type=F32)
        a_ref[:, sl] = (gate * jax.nn.sigmoid(gate) * up).astype(BF16)
    return x + 0.5 * jnp.dot(a_ref[...], wd_ref[...], preferred_element_type=F32)


def _layer_spec(w, idx):
    tail = w.shape[1:]
    return _resident((None,) + tail, lambda i: (idx,) + (0,) * len(tail))


def _ffn_specs(ffn, layer):
    return [_layer_spec(w, layer) for w in ffn]


def _rope(y, cos, sin_signed):
    lane = lax.broadcasted_iota(jnp.int32, y.shape, 1)
    first_half = (lane & (HEAD_DIM // 2)) == 0
    swapped = jnp.where(first_half, pltpu.roll(y, LANES - HEAD_DIM // 2, 1),
                        pltpu.roll(y, HEAD_DIM // 2, 1))
    return y * cos + swapped * sin_signed


A_GROUP = A_HEADS // A_KV_HEADS
A_HEAD_ORDER = (0, 3, 1, 4, 2, 5, 6, 9, 7, 10, 8, 11)
_A_Q, _A_KV = A_HEADS * HEAD_DIM, A_KV_HEADS * HEAD_DIM
_B_W = B_SLOTS * HEAD_DIM


def _scatter_residues(y, buf_ref, out_ref, dil, col):
    n = y.shape[0] // dil
    slabs = y.shape[1] // LANES
    for sl in range(slabs):
        buf_ref[sl] = y[:, sl * LANES:(sl + 1) * LANES]
    for r in range(dil):
        for sl in range(slabs):
            piece = buf_ref[sl, pl.ds(r, n, stride=dil), :]
            out_ref[r, :, col + sl * LANES:col + (sl + 1) * LANES] = piece.astype(out_ref.dtype)


def _inproj_ab_stage(x, g_ref, w_ref, cos_ref, sin_ref, qa_ref, ka_ref, va_ref, b_refs, buf_ref, tn):
    xn = _rms(x, g_ref[...]).astype(BF16)
    cos = cos_ref[...]
    sin = sin_ref[...]
    scale = HEAD_DIM ** -0.5
    b_base = _A_Q + 2 * _A_KV
    for c in range(w_ref.shape[1] // tn):
        col = c * tn
        y = jnp.dot(xn, w_ref[:, col:col + tn], preferred_element_type=F32)
        if col < b_base:
            is_q, is_v = col < _A_Q, col >= _A_Q + _A_KV
        else:
            part = ((col - b_base) % (3 * _B_W)) // _B_W
            is_q, is_v = part == 0, part == 2
        if not is_v:
            y = jnp.concatenate(
                [_rope(y[:, j:j + LANES], cos, sin) for j in range(0, tn, LANES)], axis=1)
        if is_q:
            y = y * scale
        if col < _A_Q:
            qa_ref[:, col:col + tn] = y.astype(BF16)
        elif col < _A_Q + _A_KV:
            ka_ref[:, col - _A_Q:col - _A_Q + tn] = y.astype(BF16)
        elif col < b_base:
            va_ref[:, col - _A_Q - _A_KV:col - _A_Q - _A_KV + tn] = y.astype(BF16)
        else:
            gi, off = divmod(col - b_base, 3 * _B_W)
            dil = B_PAIRS[gi][1]
            if dil == 1:
                b_refs[gi][:, off:off + tn] = y.astype(BF16)
            else:
                _scatter_residues(y, buf_ref, b_refs[gi], dil, off)


def _pre_ab_kernel(h_ref, g1_ref, wg_ref, wu_ref, wd_ref, gm_ref, w_ref, cos_ref, sin_ref,
                   h1_ref, qa_ref, ka_ref, va_ref, b0_ref, b1_ref, b2_ref, a_ref, buf_ref, *, tf, tn):
    x = _ffn_stage(h_ref[...], g1_ref, wg_ref, wu_ref, wd_ref, a_ref, tf)
    h1_ref[...] = x
    _inproj_ab_stage(x, gm_ref, w_ref, cos_ref, sin_ref, qa_ref, ka_ref, va_ref,
                     (b0_ref, b1_ref, b2_ref), buf_ref, tn)


def _pre_ab(h, ffn, g_mix, w_in, cos, sin, layer, j, bsz, *, tm=512, tf=256, tn=256):
    t, d = h.shape
    s = t // bsz
    nts = s // tm
    row = lambda i: (i, 0)
    tab = lambda i: (i % nts, 0)
    res = lambda i: (i // nts, 0, i % nts, 0)
    widths = (d, _A_Q, _A_KV, _A_KV, 3 * _B_W)
    out_specs = [pl.BlockSpec((tm, w), row) for w in widths]
    out_shape = [jax.ShapeDtypeStruct((t, w), F32 if k == 0 else BF16) for k, w in enumerate(widths)]
    for _, dil in B_PAIRS[1:]:
        out_specs.append(pl.BlockSpec((None, dil, tm // dil, 3 * _B_W), res))
        out_shape.append(jax.ShapeDtypeStruct((bsz, dil, s // dil, 3 * _B_W), BF16))
    return pl.pallas_call(
        functools.partial(_pre_ab_kernel, tf=tf, tn=tn),
        grid=(t // tm,),
        in_specs=[pl.BlockSpec((tm, d), row)] + _ffn_specs(ffn, layer)
        + [_layer_spec(g_mix, layer), _layer_spec(w_in, j),
           pl.BlockSpec((tm, LANES), tab), pl.BlockSpec((tm, LANES), tab)],
        out_specs=out_specs,
        out_shape=out_shape,
        scratch_shapes=[pltpu.VMEM((tm, ffn[1].shape[-1]), BF16),
                        pltpu.VMEM((tn // LANES, tm, LANES), F32)],
        compiler_params=_params(1),
        name="pre_ab",
    )(h, *ffn, g_mix, w_in, cos, sin)


def _band_kernel(*refs, radius, qpk, length, use_sink, want_lse, tq, n_sub, unroll):
    refs = list(refs)
    sink_ref = refs.pop(0) if use_sink else None
    q_ref, k_ref, v_ref, o_ref = refs[:4]
    lse_ref = refs[4] if want_lse else None
    win = tq + 2 * radius
    n_kg = k_ref.shape[1] // LANES
    nb = 2 * qpk
    t = pl.program_id(2)
    left = lax.broadcasted_iota(jnp.int32, (tq, LANES), 1) < HEAD_DIM
    zero = jnp.zeros((tq, LANES), BF16)
    rel0 = (lax.broadcasted_iota(jnp.int32, (tq, win), 1)
            - lax.broadcasted_iota(jnp.int32, (tq, win), 0))

    def body(jj, carry):
        jobs = []
        for u in range(unroll):
            sub = jj * unroll + u
            rows = pl.ds(pl.multiple_of(sub * tq, tq), tq)
            qpos0 = (t * n_sub + sub) * tq
            kstart = pl.multiple_of(jnp.clip(qpos0 - radius, 0, length - win), radius)
            bias = jnp.where(jnp.abs(rel0 + (kstart - qpos0)) <= radius, 0.0, NEG_INF)
            jobs += [(rows, pl.ds(kstart, win), bias, kg) for kg in range(n_kg)]
        scores = []
        for rows, keys, bias, kg in jobs:
            blocks = []
            for i in range(qpk):
                p = kg * qpk + i
                qp = q_ref[rows, p * LANES:(p + 1) * LANES]
                blocks += [jnp.where(left, qp, zero), jnp.where(left, zero, qp)]
            scores.append(lax.dot_general(
                jnp.concatenate(blocks, axis=0), k_ref[keys, kg * LANES:(kg + 1) * LANES],
                (((1,), (1,)), ((), ())), preferred_element_type=F32))
        stats = []
        for (rows, keys, bias, kg), s in zip(jobs, scores):
            es, dens, ms = [], [], []
            for b in range(nb):
                sb = s[b * tq:(b + 1) * tq] + bias
                mb = jnp.max(sb, axis=-1, keepdims=True)
                if use_sink:
                    sink = sink_ref[kg * nb + b]
                    mb = jnp.maximum(mb, sink)
                eb = jnp.exp(sb - mb)
                db = jnp.sum(eb, axis=-1, keepdims=True)
                if use_sink:
                    db = db + jnp.exp(sink - mb)
                es.append(eb.astype(BF16))
                dens.append(db)
                ms.append(mb)
            stats.append((jnp.concatenate(es, axis=0), dens, ms))
        for (rows, keys, bias, kg), (e, dens, ms) in zip(jobs, stats):
            o2 = jnp.dot(e, v_ref[keys, kg * LANES:(kg + 1) * LANES], preferred_element_type=F32)
            for i in range(qpk):
                p = kg * qpk + i
                even = o2[(2 * i) * tq:(2 * i + 1) * tq] / dens[2 * i]
                odd = o2[(2 * i + 1) * tq:(2 * i + 2) * tq] / dens[2 * i + 1]
                o_ref[rows, p * LANES:(p + 1) * LANES] = jnp.where(left, even, odd).astype(o_ref.dtype)
                if want_lse:
                    lse_e = ms[2 * i] + jnp.log(dens[2 * i])
                    lse_o = ms[2 * i + 1] + jnp.log(dens[2 * i + 1])
                    lse_ref[rows, p * LANES:(p + 1) * LANES] = jnp.where(
                        left, jnp.broadcast_to(lse_e, (tq, LANES)), jnp.broadcast_to(lse_o, (tq, LANES)))
        return carry

    lax.fori_loop(0, n_sub // unroll, body, 0)


def _banded_attention(q, k, v, *, q_col, k_col, v_col, radius, qpk, sink=None, want_lse=False,
                      name):
    bsz, dil, length, _ = q.shape
    tq = ATT_BLOCK
    kw = 2 * LANES
    qw = kw * qpk
    tile = min(length, 1024)
    n_sub = tile // tq
    in_specs = [
        pl.BlockSpec((None, None, tile, qw), lambda b, r, t: (b, r, t, q_col)),
        pl.BlockSpec((None, None, length, kw), lambda b, r, t: (b, r, 0, k_col)),
        pl.BlockSpec((None, None, length, kw), lambda b, r, t: (b, r, 0, v_col)),
    ]
    args = [q, k, v]
    if sink is not None:
        in_specs.insert(0, pl.BlockSpec(memory_space=pltpu.SMEM))
        args.insert(0, sink)
    out_map = lambda b, r, t: (b, r, t, 0)
    out_specs = [pl.BlockSpec((None, None, tile, qw), out_map)]
    out_shape = [jax.ShapeDtypeStruct((bsz, dil, length, qw), BF16)]
    if want_lse:
        out_specs.append(pl.BlockSpec((None, None, tile, qw), out_map))
        out_shape.append(jax.ShapeDtypeStruct((bsz, dil, length, qw), F32))
    return pl.pallas_call(
        functools.partial(_band_kernel, radius=radius, qpk=qpk, length=length,
                          use_sink=sink is not None, want_lse=want_lse, tq=tq, n_sub=n_sub,
                          unroll=max(1, min(n_sub, 4 // qpk))),
        grid=(bsz, dil, length // tile),
        in_specs=in_specs,
        out_specs=out_specs,
        out_shape=out_shape,
        compiler_params=_params(3),
        name=name,
    )(*args)


def _interleave_residues(src_ref, buf_ref, dil):
    n = src_ref.shape[1]
    slabs = src_ref.shape[2] // LANES
    for r in range(dil):
        x = src_ref[r].astype(F32)
        for sl in range(slabs):
            buf_ref[sl, pl.ds(r, n, stride=dil), :] = x[:, sl * LANES:(sl + 1) * LANES]
    return [buf_ref[sl] for sl in range(slabs)]


def _outproj_ab_stage(x, oa_ref, o0_ref, l0_ref, o1_ref, l1_ref, o2_ref, l2_ref, w_ref, buf_ref):
    dils = [dil for _, dil in B_PAIRS]
    o1 = _interleave_residues(o1_ref, buf_ref.at[0], dils[1])
    l1 = _interleave_residues(l1_ref, buf_ref.at[1], dils[1])
    o2 = _interleave_residues(o2_ref, buf_ref.at[2], dils[2])
    l2 = _interleave_residues(l2_ref, buf_ref.at[3], dils[2])
    merged = []
    for sl in range(len(o1)):
        lanes = slice(sl * LANES, (sl + 1) * LANES)
        l0 = l0_ref[:, lanes]
        m = jnp.maximum(jnp.maximum(l0, l1[sl]), l2[sl])
        e0, e1, e2 = jnp.exp(l0 - m), jnp.exp(l1[sl] - m), jnp.exp(l2[sl] - m)
        ob = (e0 * o0_ref[:, lanes].astype(F32) + e1 * o1[sl] + e2 * o2[sl]) / (e0 + e1 + e2)
        merged.append(ob.astype(BF16))
    na = oa_ref.shape[1]
    y = jnp.dot(oa_ref[...], w_ref[:na, :], preferred_element_type=F32)
    return x + y + jnp.dot(jnp.concatenate(merged, axis=1), w_ref[na:, :], preferred_element_type=F32)


def _ple_stage(x, g_ref, wg_ref, p_ref, wp_ref):
    xn = _rms(x, g_ref[...]).astype(BF16)
    gate = jax.nn.sigmoid(jnp.dot(xn, wg_ref[...], preferred_element_type=F32))
    proj = jnp.dot(p_ref[...].astype(BF16), wp_ref[...], preferred_element_type=F32)
    return x + gate * proj


def _tail_stages(x, g2_ref, wg_ref, wu_ref, wd_ref, gp_ref, wpg_ref, p_ref, wpp_ref, gf_ref, a_ref,
                 tf, final):
    x = _ffn_stage(x, g2_ref, wg_ref, wu_ref, wd_ref, a_ref, tf)
    x = _ple_stage(x, gp_ref, wpg_ref, p_ref, wpp_ref)
    return _rms(x, gf_ref[...]) if final else x


def _post_ab_kernel(h_ref, oa_ref, o0_ref, l0_ref, o1_ref, l1_ref, o2_ref, l2_ref, wo_ref, g2_ref,
                    wg_ref, wu_ref, wd_ref, gp_ref, wpg_ref, p_ref, wpp_ref, gf_ref, out_ref,
                    a_ref, buf_ref, *, tf, final):
    x = _outproj_ab_stage(h_ref[...], oa_ref, o0_ref, l0_ref, o1_ref, l1_ref, o2_ref, l2_ref,
                          wo_ref, buf_ref)
    out_ref[...] = _tail_stages(x, g2_ref, wg_ref, wu_ref, wd_ref, gp_ref, wpg_ref, p_ref, wpp_ref,
                                gf_ref, a_ref, tf, final)


def _tail_specs(ffn, ple, p, g_final, layer, tm):
    g_ple, w_gate, w_proj = ple
    return (_ffn_specs(ffn, layer)
            + [_layer_spec(g_ple, layer), _layer_spec(w_gate, layer),
               pl.BlockSpec((None, tm, p.shape[-1]), lambda i: (layer, i, 0)),
               _layer_spec(w_proj, layer), _resident(g_final.shape, lambda i: (0, 0))])


def _post_ab(h, oa, obs, lses, w_out, ffn, ple, p, g_final, layer, j, bsz, *, final, tm=512, tf=256):
    t, d = h.shape
    s = t // bsz
    nts = s // tm
    row = lambda i: (i, 0)
    res = lambda i: (i // nts, 0, i % nts, 0)
    nb = obs[0].shape[-1]
    in_specs = [pl.BlockSpec((tm, d), row), pl.BlockSpec((tm, oa.shape[1]), row),
                pl.BlockSpec((tm, nb), row), pl.BlockSpec((tm, nb), row)]
    args = [h, oa, obs[0], lses[0]]
    for (_, dil), o, lse in zip(B_PAIRS[1:], obs[1:], lses[1:]):
        in_specs += [pl.BlockSpec((None, dil, tm // dil, nb), res)] * 2
        args += [o, lse]
    in_specs += [_layer_spec(w_out, j)] + _tail_specs(ffn, ple, p, g_final, layer, tm)
    g_ple, w_gate, w_proj = ple
    return pl.pallas_call(
        functools.partial(_post_ab_kernel, tf=tf, final=final),
        grid=(t // tm,),
        in_specs=in_specs,
        out_specs=pl.BlockSpec((tm, d), row),
        out_shape=jax.ShapeDtypeStruct((t, d), F32),
        scratch_shapes=[pltpu.VMEM((tm, ffn[1].shape[-1]), BF16),
                        pltpu.VMEM((4, nb // LANES, tm, LANES), F32)],
        compiler_params=_params(1),
        name="post_ab",
    )(*args, w_out, *ffn, g_ple, w_gate, p, w_proj, g_final)


def _inproj_c_stage(x, g_ref, w_ref, q_ref, k_ref, v_ref, tn):
    xn = _rms(x, g_ref[...]).astype(BF16)
    width = q_ref.shape[1]
    scale = HEAD_DIM ** -0.5
    for c in range(w_ref.shape[1] // tn):
        col = c * tn
        y = jnp.dot(xn, w_ref[:, col:col + tn], preferred_element_type=F32)
        if col < width:
            q_ref[:, col:col + tn] = (y * scale).astype(BF16)
        elif col < 2 * width:
            k_ref[:, col - width:col - width + tn] = y.astype(BF16)
        else:
            v_ref[:, col - 2 * width:col - 2 * width + tn] = y.astype(BF16)


def _pre_c_kernel(h_ref, g1_ref, wg_ref, wu_ref, wd_ref, gm_ref, w_ref, h1_ref, q_ref, k_ref, v_ref,
                  a_ref, *, tf, tn):
    x = _ffn_stage(h_ref[...], g1_ref, wg_ref, wu_ref, wd_ref, a_ref, tf)
    h1_ref[...] = x
    _inproj_c_stage(x, gm_ref, w_ref, q_ref, k_ref, v_ref, tn)


def _pre_c(h, ffn, g_mix, w_in, layer, j, *, tm=512, tf=256, tn=256):
    t, d = h.shape
    n = w_in.shape[-1]
    row = lambda i: (i, 0)
    return pl.pallas_call(
        functools.partial(_pre_c_kernel, tf=tf, tn=tn),
        grid=(t // tm,),
        in_specs=[pl.BlockSpec((tm, d), row)] + _ffn_specs(ffn, layer)
        + [_layer_spec(g_mix, layer), _layer_spec(w_in, j)],
        out_specs=[pl.BlockSpec((tm, d), row)] + [pl.BlockSpec((tm, n // 3), row)] * 3,
        out_shape=[jax.ShapeDtypeStruct((t, d), F32)] + [jax.ShapeDtypeStruct((t, n // 3), BF16)] * 3,
        scratch_shapes=[pltpu.VMEM((tm, ffn[1].shape[-1]), BF16)],
        compiler_params=_params(1),
        name="pre_c",
    )(h, *ffn, g_mix, w_in)


def _na_bias_kernel(rpb_ref, o_ref, *, kh):
    h = pl.program_id(0)
    n_dr, n_dc = rpb_ref.shape[1], rpb_ref.shape[2]
    j = lax.broadcasted_iota(jnp.int32, (GRID_W, LANES), 0)
    lane = lax.broadcasted_iota(jnp.int32, (GRID_W, LANES), 1)
    right = lane >= GRID_W
    c = lane & (GRID_W - 1)
    start = jnp.clip(j - NA_KW // 2, 0, GRID_W - NA_KW)
    inside = (c >= start) & (c < start + NA_KW)
    dc = c - j + (NA_KW - 1)
    pairs = []
    for dr in range(n_dr - 1):
        acc = jnp.full((GRID_W, LANES), NEG_INF, F32)
        for t in range(n_dc):
            acc = jnp.where(dc == t, jnp.where(right, rpb_ref[h, dr + 1, t], rpb_ref[h, dr, t]), acc)
        pairs.append(jnp.where(inside, acc, NEG_INF))
    for off in range(o_ref.shape[0]):
        for a in range(0, kh, 2):
            o_ref[off, :, a * GRID_W:(a + 2) * GRID_W] = pairs[off + a]


def _na_bias(rpb, kh):
    nh = rpb.shape[0]
    return pl.pallas_call(
        functools.partial(_na_bias_kernel, kh=kh),
        grid=(nh,),
        in_specs=[pl.BlockSpec(memory_space=pltpu.SMEM)],
        out_specs=pl.BlockSpec((NA_KH, None, GRID_W, kh * GRID_W), lambda h: (0, h, 0, 0)),
        out_shape=jax.ShapeDtypeStruct((NA_KH, nh, GRID_W, kh * GRID_W), F32),
        compiler_params=_params(1),
        name="na_bias",
    )(rpb)


def _na_kernel(q_ref, k_ref, v_ref, tb_ref, o_ref, *, rows_per_step, rows, kh):
    ig = pl.program_id(2)
    left = lax.broadcasted_iota(jnp.int32, (GRID_W, LANES), 1) < HEAD_DIM
    zero = jnp.zeros((GRID_W, LANES), BF16)

    windows, scores = [], []
    for g in range(rows_per_step):
        i = ig * rows_per_step + g
        rs = jnp.clip(i - kh // 2, 0, rows - kh)
        off = rs - i + (NA_KH - 1)
        win = pl.ds(pl.multiple_of(rs * GRID_W, GRID_W), kh * GRID_W)
        qi = q_ref[g * GRID_W:(g + 1) * GRID_W, :]
        q2 = jnp.concatenate([jnp.where(left, qi, zero), jnp.where(left, zero, qi)], axis=0)
        s = lax.dot_general(q2, k_ref[win, :], (((1,), (1,)), ((), ())), preferred_element_type=F32)
        scores.append(s + tb_ref[off])
        windows.append(win)
    probs, dens = [], []
    for s in scores:
        e = jnp.exp(s - jnp.max(s, axis=-1, keepdims=True))
        dens.append(jnp.sum(e, axis=-1, keepdims=True))
        probs.append(e.astype(BF16))
    for g in range(rows_per_step):
        o2 = jnp.dot(probs[g], v_ref[windows[g], :], preferred_element_type=F32) / dens[g]
        o_ref[g * GRID_W:(g + 1) * GRID_W, :] = jnp.where(
            left, o2[:GRID_W], o2[GRID_W:]).astype(o_ref.dtype)


def _neighbourhood_attention(q, k, v, bias, *, rows_per_step=8):
    bsz, s, width = q.shape
    rows = s // GRID_W
    kh = min(NA_KH, rows)
    n_pairs = width // LANES
    g = rows_per_step
    return pl.pallas_call(
        functools.partial(_na_kernel, rows_per_step=g, rows=rows, kh=kh),
        grid=(bsz, n_pairs, rows // g),
        in_specs=[
            pl.BlockSpec((None, g * GRID_W, LANES), lambda b, p, i: (b, i, p)),
            pl.BlockSpec((None, s, LANES), lambda b, p, i: (b, 0, p)),
            pl.BlockSpec((None, s, LANES), lambda b, p, i: (b, 0, p)),
            pl.BlockSpec((bias.shape[0], None) + bias.shape[2:], lambda b, p, i: (0, p, 0, 0)),
        ],
        out_specs=pl.BlockSpec((None, g * GRID_W, LANES), lambda b, p, i: (b, i, p)),
        out_shape=jax.ShapeDtypeStruct((bsz, s, width), BF16),
        compiler_params=_params(3),
        name="na_attention",
    )(q, k, v, bias)


def _post_c_kernel(h_ref, o_ref, wo_ref, g2_ref, wg_ref, wu_ref, wd_ref, gp_ref, wpg_ref, p_ref,
                   wpp_ref, gf_ref, out_ref, a_ref, *, tf, final):
    x = h_ref[...] + jnp.dot(o_ref[...], wo_ref[...], preferred_element_type=F32)
    out_ref[...] = _tail_stages(x, g2_ref, wg_ref, wu_ref, wd_ref, gp_ref, wpg_ref, p_ref, wpp_ref,
                                gf_ref, a_ref, tf, final)


def _post_c(h, o, w_out, ffn, ple, p, g_final, layer, j, *, final, tm=512, tf=256):
    t, d = h.shape
    row = lambda i: (i, 0)
    g_ple, w_gate, w_proj = ple
    return pl.pallas_call(
        functools.partial(_post_c_kernel, tf=tf, final=final),
        grid=(t // tm,),
        in_specs=[pl.BlockSpec((tm, d), row), pl.BlockSpec((tm, o.shape[1]), row),
                  _layer_spec(w_out, j)] + _tail_specs(ffn, ple, p, g_final, layer, tm),
        out_specs=pl.BlockSpec((tm, d), row),
        out_shape=jax.ShapeDtypeStruct((t, d), F32),
        scratch_shapes=[pltpu.VMEM((tm, ffn[1].shape[-1]), BF16)],
        compiler_params=_params(1),
        name="post_c",
    )(h, o, w_out, *ffn, g_ple, w_gate, p, w_proj, g_final)


def _rope_tables(s):
    half = HEAD_DIM // 2
    inv = ROPE_THETA ** (-jnp.arange(half, dtype=F32) / half)
    ang = jnp.arange(s, dtype=jnp.int32).astype(F32)[:, None] * inv[None, :]
    cos, sin = jnp.cos(ang), jnp.sin(ang)
    reps = LANES // HEAD_DIM
    return (jnp.tile(jnp.concatenate([cos, cos], axis=1), (1, reps)),
            jnp.tile(jnp.concatenate([-sin, sin], axis=1), (1, reps)))


def _permute_ab_params(w_in, w_out, sink):
    heads = lambda w, axis: [lax.slice_in_dim(w, h * HEAD_DIM, (h + 1) * HEAD_DIM, axis=axis)
                             for h in A_HEAD_ORDER]
    b_base = _A_Q + 2 * _A_KV
    n_groups = len(B_PAIRS)
    groups = [w_in[..., b_base + (part * n_groups + gi) * _B_W:b_base + (part * n_groups + gi + 1) * _B_W]
              for gi in range(n_groups) for part in range(3)]
    w_in = jnp.concatenate(heads(w_in, 2) + [w_in[..., _A_Q:b_base]] + groups, axis=-1)
    w_out = jnp.concatenate(heads(w_out, 1) + [w_out[:, _A_Q:, :]], axis=1)
    return w_in, w_out, sink[:, jnp.array(A_HEAD_ORDER)]


def _ab_attention(qa, ka, va, b0, b1, b2, sink, bsz, s):
    seq = lambda a: a.reshape(bsz, 1, s, a.shape[-1])
    oa = _banded_attention(seq(qa), seq(ka), seq(va), q_col=0, k_col=0, v_col=0, radius=A_RADIUS,
                           qpk=A_GROUP, sink=sink, name="band_a")[0]
    obs, lses = [], []
    for gi, (qkv, (window, dil)) in enumerate(zip((seq(b0), b1, b2), B_PAIRS)):
        o, lse = _banded_attention(qkv, qkv, qkv, q_col=0, k_col=1, v_col=2,
                                   radius=window // (2 * dil), qpk=1, want_lse=True,
                                   name=f"band_b{gi}")
        obs.append(o)
        lses.append(lse)
    flat = lambda a: a.reshape(bsz * s, a.shape[-1])
    obs[0], lses[0] = flat(obs[0]), flat(lses[0])
    return flat(oa), obs, lses


def _c_attention(q, k, v, rpb, bsz, s):
    kh = min(NA_KH, s // GRID_W)
    bias = _na_bias(rpb, kh)
    bias = bias.reshape(NA_KH, bias.shape[1] // 2, 2 * GRID_W, kh * GRID_W)
    o = _neighbourhood_attention(q.reshape(bsz, s, -1), k.reshape(bsz, s, -1),
                                 v.reshape(bsz, s, -1), bias)
    return o.reshape(bsz * s, -1)


def kernel(x, p, norm_ffn1, ffn1_w_gate, ffn1_w_up, ffn1_w_down, norm_mix, w_in_ab, sink_a, w_out_ab, w_in_c, rpb_c, w_out_c, norm_ffn2, ffn2_w_gate, ffn2_w_up, ffn2_w_down, norm_ple, w_ple_gate, w_ple_proj, norm_final):
    bsz, s, d = x.shape
    depth = p.shape[0]
    bf = lambda w: w.astype(BF16)
    gains = lambda g: g.reshape(g.shape[0], 1, d)
    ffn1 = (gains(norm_ffn1), bf(ffn1_w_gate), bf(ffn1_w_up), bf(ffn1_w_down))
    ffn2 = (gains(norm_ffn2), bf(ffn2_w_gate), bf(ffn2_w_up), bf(ffn2_w_down))
    ple = (gains(norm_ple), bf(w_ple_gate), bf(w_ple_proj))
    g_mix = gains(norm_mix)
    w_in_ab, w_out_ab, w_in_c, w_out_c = bf(w_in_ab), bf(w_out_ab), bf(w_in_c), bf(w_out_c)
    w_in_ab, w_out_ab, sink_a = _permute_ab_params(w_in_ab, w_out_ab, sink_a)
    p2 = p.reshape(depth, bsz * s, p.shape[-1])
    g_final = norm_final.reshape(1, d)
    cos, sin = _rope_tables(s)

    h = x.reshape(bsz * s, d)
    for i in range(depth):
        j = i // 2
        final = i == depth - 1
        if i % 2 == 0:
            h, *proj = _pre_ab(h, ffn1, g_mix, w_in_ab, cos, sin, i, j, bsz)
            oa, obs, lses = _ab_attention(*proj, sink_a[j], bsz, s)
            h = _post_ab(h, oa, obs, lses, w_out_ab, ffn2, ple, p2, g_final, i, j, bsz, final=final)
        else:
            h, q, k, v = _pre_c(h, ffn1, g_mix, w_in_c, i, j)
            o = _c_attention(q, k, v, rpb_c[j], bsz, s)
            h = _post_c(h, o, w_out_c, ffn2, ple, p2, g_final, i, j, final=final)
    return h.reshape(bsz, s, d)
```

```python
import functools

import jax
import jax.numpy as jnp
from jax import lax
from jax.experimental import pallas as pl
from jax.experimental.pallas import tpu as pltpu

HEAD_DIM = 64
GRID_W = 64
ROPE_THETA = 10000.0
RMS_EPS = 1e-6
A_HEADS = 12
A_KV_HEADS = 4
A_RADIUS = 128
B_PAIRS = ((128, 1), (512, 4), (2048, 16))
B_SLOTS = 4
NA_KH = 8
NA_KW = 16
NEG_INF = -1e30

LANES = 128
ATT_BLOCK = 128
VMEM_LIMIT = 56 * 1024 * 1024
F32 = jnp.float32
BF16 = jnp.bfloat16


def _params(n_axes):
    return pltpu.CompilerParams(dimension_semantics=("arbitrary",) * n_axes,
                                vmem_limit_bytes=VMEM_LIMIT)


def _resident(block_shape, index_map):
    return pl.BlockSpec(block_shape, index_map, pipeline_mode=pl.Buffered(1))


def _rms(x, g):
    ms = jnp.mean(x * x, axis=-1, keepdims=True)
    return x * lax.rsqrt(ms + RMS_EPS) * g


CHUNK = 256


def _n_chunks(w, axis):
    return w.shape[axis] // CHUNK


def _chunk_spec(w, idx, axis, block_of=lambda c: c):
    last = _n_chunks(w, axis) - 1
    blk = lambda i: block_of(jnp.minimum(i, last))
    if axis == 1:
        return pl.BlockSpec((None, CHUNK, w.shape[2]), lambda i: (idx, blk(i), 0),
                            pipeline_mode=pl.Buffered(1))
    return pl.BlockSpec((None, w.shape[1], CHUNK), lambda i: (idx, 0, blk(i)),
                        pipeline_mode=pl.Buffered(1))


def _chunk_scratch(w, axis, n=None):
    n = _n_chunks(w, axis) if n is None else n
    shape = (n, CHUNK, w.shape[2]) if axis == 1 else (n, w.shape[1], CHUNK)
    return pltpu.VMEM(shape, BF16)


def _stage_chunks(i, pairs):
    for slab_ref, dst_ref, first, n in pairs:
        dst_ref[first + jnp.minimum(i, n - 1)] = slab_ref[...].astype(BF16)


def _rows(chunks_ref, lo=0, hi=None):
    w = chunks_ref[lo:hi]
    return w.reshape(w.shape[0] * w.shape[1], w.shape[2])


def _ffn_stage(x, g_ref, wg_s, wu_s, wd_s, a_ref):
    xn = _rms(x, g_ref[...]).astype(BF16)
    for f in range(wg_s.shape[0]):
        gate = jnp.dot(xn, wg_s[f], preferred_element_type=F32)
        up = jnp.dot(xn, wu_s[f], preferred_element_type=F32)
        a_ref[:, f * CHUNK:(f + 1) * CHUNK] = (gate * jax.nn.sigmoid(gate) * up).astype(BF16)
    return x + 0.5 * jnp.dot(a_ref[...], _rows(wd_s), preferred_element_type=F32)


def _layer_spec(w, idx):
    tail = w.shape[1:]
    return _resident((None,) + tail, lambda i: (idx,) + (0,) * len(tail))


def _ffn_specs(ffn, layer):
    gain, w_gate, w_up, w_down = ffn
    return [_layer_spec(gain, layer), _chunk_spec(w_gate, layer, 2), _chunk_spec(w_up, layer, 2),
            _chunk_spec(w_down, layer, 1)]


def _ffn_scratch(ffn):
    _, w_gate, w_up, w_down = ffn
    return [_chunk_scratch(w_gate, 2), _chunk_scratch(w_up, 2), _chunk_scratch(w_down, 1)]


def _ffn_staging(ffn, refs, scratch):
    _, w_gate, w_up, w_down = ffn
    ns = (_n_chunks(w_gate, 2), _n_chunks(w_up, 2), _n_chunks(w_down, 1))
    return [(r, s, 0, n) for r, s, n in zip(refs, scratch, ns)]


def _rope(y, cos, sin_signed):
    lane = lax.broadcasted_iota(jnp.int32, y.shape, 1)
    first_half = (lane & (HEAD_DIM // 2)) == 0
    swapped = jnp.where(first_half, pltpu.roll(y, LANES - HEAD_DIM // 2, 1),
                        pltpu.roll(y, HEAD_DIM // 2, 1))
    return y * cos + swapped * sin_signed


A_GROUP = A_HEADS // A_KV_HEADS
A_HEAD_ORDER = (0, 3, 1, 4, 2, 5, 6, 9, 7, 10, 8, 11)
_A_Q, _A_KV = A_HEADS * HEAD_DIM, A_KV_HEADS * HEAD_DIM
_B_W = B_SLOTS * HEAD_DIM


def _scatter_residues(y, buf_ref, out_ref, dil, col):
    n = y.shape[0] // dil
    slabs = y.shape[1] // LANES
    for sl in range(slabs):
        buf_ref[sl] = y[:, sl * LANES:(sl + 1) * LANES]
    for r in range(dil):
        for sl in range(slabs):
            piece = buf_ref[sl, pl.ds(r, n, stride=dil), :]
            out_ref[r, :, col + sl * LANES:col + (sl + 1) * LANES] = piece.astype(out_ref.dtype)


def _inproj_ab_stage(x, g_ref, w_s, cos_ref, sin_ref, qa_ref, ka_ref, va_ref, b_refs, buf_ref):
    xn = _rms(x, g_ref[...]).astype(BF16)
    cos = cos_ref[...]
    sin = sin_ref[...]
    scale = HEAD_DIM ** -0.5
    b_base = _A_Q + 2 * _A_KV
    tn = CHUNK
    for c in range(w_s.shape[0]):
        col = c * tn
        y = jnp.dot(xn, w_s[c], preferred_element_type=F32)
        if col < b_base:
            is_q, is_v = col < _A_Q, col >= _A_Q + _A_KV
        else:
            part = ((col - b_base) % (3 * _B_W)) // _B_W
            is_q, is_v = part == 0, part == 2
        if not is_v:
            y = jnp.concatenate(
                [_rope(y[:, j:j + LANES], cos, sin) for j in range(0, tn, LANES)], axis=1)
        if is_q:
            y = y * scale
        if col < _A_Q:
            qa_ref[:, col:col + tn] = y.astype(BF16)
        elif col < _A_Q + _A_KV:
            ka_ref[:, col - _A_Q:col - _A_Q + tn] = y.astype(BF16)
        elif col < b_base:
            va_ref[:, col - _A_Q - _A_KV:col - _A_Q - _A_KV + tn] = y.astype(BF16)
        else:
            gi, off = divmod(col - b_base, 3 * _B_W)
            dil = B_PAIRS[gi][1]
            if dil == 1:
                b_refs[gi][:, off:off + tn] = y.astype(BF16)
            else:
                _scatter_residues(y, buf_ref, b_refs[gi], dil, off)


def _pre_ab_kernel(h_ref, g1_ref, wg_ref, wu_ref, wd_ref, gm_ref, wqa_ref, w_ref, cos_ref, sin_ref,
                   h1_ref, qa_ref, ka_ref, va_ref, b0_ref, b1_ref, b2_ref,
                   wg_s, wu_s, wd_s, w_s, a_ref, buf_ref, *, n_stage, staging):
    i = pl.program_id(0)

    @pl.when(i < n_stage)
    def _():
        _stage_chunks(i, staging((wg_ref, wu_ref, wd_ref, wqa_ref, w_ref), (wg_s, wu_s, wd_s, w_s, w_s)))

    @pl.when(i >= n_stage)
    def _():
        x = _ffn_stage(h_ref[...], g1_ref, wg_s, wu_s, wd_s, a_ref)
        h1_ref[...] = x
        _inproj_ab_stage(x, gm_ref, w_s, cos_ref, sin_ref, qa_ref, ka_ref, va_ref,
                         (b0_ref, b1_ref, b2_ref), buf_ref)


def _ab_source_block(c):
    first_b = (_A_Q + 2 * _A_KV) // CHUNK
    n_groups = len(B_PAIRS)
    g, part = (c - first_b) // 3, (c - first_b) % 3
    return jnp.where(c < first_b, c, first_b + part * n_groups + g)


def _pre_ab(h, ffn, g_mix, w_qa, w_in, cos, sin, layer, j, bsz, *, tm=512):
    t, d = h.shape
    s = t // bsz
    nts = s // tm
    n_qa = _n_chunks(w_qa, 2)
    n_in = _n_chunks(w_in, 2)
    n_stage = max(n_in, _n_chunks(ffn[1], 2))
    tile = lambda i: jnp.maximum(i - n_stage, 0)
    row = lambda i: (tile(i), 0)
    tab = lambda i: (tile(i) % nts, 0)
    res = lambda i: (tile(i) // nts, 0, tile(i) % nts, 0)
    widths = (d, _A_Q, _A_KV, _A_KV, 3 * _B_W)
    out_specs = [pl.BlockSpec((tm, w), row) for w in widths]
    out_shape = [jax.ShapeDtypeStruct((t, w), F32 if k == 0 else BF16) for k, w in enumerate(widths)]
    for _, dil in B_PAIRS[1:]:
        out_specs.append(pl.BlockSpec((None, dil, tm // dil, 3 * _B_W), res))
        out_shape.append(jax.ShapeDtypeStruct((bsz, dil, s // dil, 3 * _B_W), BF16))

    def staging(refs, scratch):
        return _ffn_staging(ffn, refs[:3], scratch[:3]) + [
            (refs[4], scratch[4], 0, n_in), (refs[3], scratch[3], 0, n_qa)]

    return pl.pallas_call(
        functools.partial(_pre_ab_kernel, n_stage=n_stage, staging=staging),
        grid=(n_stage + t // tm,),
        in_specs=[pl.BlockSpec((tm, d), row)] + _ffn_specs(ffn, layer)
        + [_layer_spec(g_mix, layer), _chunk_spec(w_qa, j, 2),
           _chunk_spec(w_in, j, 2, _ab_source_block),
           pl.BlockSpec((tm, LANES), tab), pl.BlockSpec((tm, LANES), tab)],
        out_specs=out_specs,
        out_shape=out_shape,
        scratch_shapes=_ffn_scratch(ffn) + [_chunk_scratch(w_in, 2),
                                            pltpu.VMEM((tm, ffn[1].shape[-1]), BF16),
                                            pltpu.VMEM((CHUNK // LANES, tm, LANES), F32)],
        compiler_params=_params(1),
        name="pre_ab",
    )(h, *ffn, g_mix, w_qa, w_in, cos, sin)


def _band_kernel(*refs, radius, qpk, length, use_sink, want_lse, tq, n_sub, unroll):
    refs = list(refs)
    sink_ref = refs.pop(0) if use_sink else None
    q_ref, k_ref, v_ref, o_ref = refs[:4]
    lse_ref = refs[4] if want_lse else None
    win = tq + 2 * radius
    n_kg = k_ref.shape[1] // LANES
    nb = 2 * qpk
    t = pl.program_id(2)
    left = lax.broadcasted_iota(jnp.int32, (tq, LANES), 1) < HEAD_DIM
    zero = jnp.zeros((tq, LANES), BF16)
    rel0 = (lax.broadcasted_iota(jnp.int32, (tq, win), 1)
            - lax.broadcasted_iota(jnp.int32, (tq, win), 0))

    def body(jj, carry):
        jobs = []
        for u in range(unroll):
            sub = jj * unroll + u
            rows = pl.ds(pl.multiple_of(sub * tq, tq), tq)
            qpos0 = (t * n_sub + sub) * tq
            kstart = pl.multiple_of(jnp.clip(qpos0 - radius, 0, length - win), radius)
            bias = jnp.where(jnp.abs(rel0 + (kstart - qpos0)) <= radius, 0.0, NEG_INF)
            jobs += [(rows, pl.ds(kstart, win), bias, kg) for kg in range(n_kg)]
        scores = []
        for rows, keys, bias, kg in jobs:
            blocks = []
            for i in range(qpk):
                p = kg * qpk + i
                qp = q_ref[rows, p * LANES:(p + 1) * LANES]
                blocks += [jnp.where(left, qp, zero), jnp.where(left, zero, qp)]
            scores.append(lax.dot_general(
                jnp.concatenate(blocks, axis=0), k_ref[keys, kg * LANES:(kg + 1) * LANES],
                (((1,), (1,)), ((), ())), preferred_element_type=F32))
        stats = []
        for (rows, keys, bias, kg), s in zip(jobs, scores):
            es, dens, ms = [], [], []
            for b in range(nb):
                sb = s[b * tq:(b + 1) * tq] + bias
                mb = jnp.max(sb, axis=-1, keepdims=True)
                if use_sink:
                    sink = sink_ref[kg * nb + b]
                    mb = jnp.maximum(mb, sink)
                eb = jnp.exp(sb - mb)
                db = jnp.sum(eb, axis=-1, keepdims=True)
                if use_sink:
                    db = db + jnp.exp(sink - mb)
                es.append(eb.astype(BF16))
                dens.append(db)
                ms.append(mb)
            stats.append((jnp.concatenate(es, axis=0), dens, ms))
        for (rows, keys, bias, kg), (e, dens, ms) in zip(jobs, stats):
            o2 = jnp.dot(e, v_ref[keys, kg * LANES:(kg + 1) * LANES], preferred_element_type=F32)
            for i in range(qpk):
                p = kg * qpk + i
                even = o2[(2 * i) * tq:(2 * i + 1) * tq] / dens[2 * i]
                odd = o2[(2 * i + 1) * tq:(2 * i + 2) * tq] / dens[2 * i + 1]
                o_ref[rows, p * LANES:(p + 1) * LANES] = jnp.where(left, even, odd).astype(o_ref.dtype)
                if want_lse:
                    lse_e = ms[2 * i] + jnp.log(dens[2 * i])
                    lse_o = ms[2 * i + 1] + jnp.log(dens[2 * i + 1])
                    lse_ref[rows, p * LANES:(p + 1) * LANES] = jnp.where(
                        left, jnp.broadcast_to(lse_e, (tq, LANES)), jnp.broadcast_to(lse_o, (tq, LANES)))
        return carry

    lax.fori_loop(0, n_sub // unroll, body, 0)


def _banded_attention(q, k, v, *, q_col, k_col, v_col, radius, qpk, sink=None, want_lse=False,
                      name):
    bsz, dil, length, _ = q.shape
    tq = ATT_BLOCK
    kw = 2 * LANES
    qw = kw * qpk
    tile = min(length, 1024)
    n_sub = tile // tq
    in_specs = [
        pl.BlockSpec((None, None, tile, qw), lambda b, r, t: (b, r, t, q_col)),
        pl.BlockSpec((None, None, length, kw), lambda b, r, t: (b, r, 0, k_col)),
        pl.BlockSpec((None, None, length, kw), lambda b, r, t: (b, r, 0, v_col)),
    ]
    args = [q, k, v]
    if sink is not None:
        in_specs.insert(0, pl.BlockSpec(memory_space=pltpu.SMEM))
        args.insert(0, sink)
    out_map = lambda b, r, t: (b, r, t, 0)
    out_specs = [pl.BlockSpec((None, None, tile, qw), out_map)]
    out_shape = [jax.ShapeDtypeStruct((bsz, dil, length, qw), BF16)]
    if want_lse:
        out_specs.append(pl.BlockSpec((None, None, tile, qw), out_map))
        out_shape.append(jax.ShapeDtypeStruct((bsz, dil, length, qw), F32))
    return pl.pallas_call(
        functools.partial(_band_kernel, radius=radius, qpk=qpk, length=length,
                          use_sink=sink is not None, want_lse=want_lse, tq=tq, n_sub=n_sub,
                          unroll=max(1, min(n_sub, 4 // qpk))),
        grid=(bsz, dil, length // tile),
        in_specs=in_specs,
        out_specs=out_specs,
        out_shape=out_shape,
        compiler_params=_params(3),
        name=name,
    )(*args)


def _interleave_residues(src_ref, buf_ref, dil):
    n = src_ref.shape[1]
    slabs = src_ref.shape[2] // LANES
    for r in range(dil):
        x = src_ref[r].astype(F32)
        for sl in range(slabs):
            buf_ref[sl, pl.ds(r, n, stride=dil), :] = x[:, sl * LANES:(sl + 1) * LANES]
    return [buf_ref[sl] for sl in range(slabs)]


def _outproj_ab_stage(x, oa_ref, o0_ref, l0_ref, o1_ref, l1_ref, o2_ref, l2_ref, wo_s, buf_ref):
    dils = [dil for _, dil in B_PAIRS]
    o1 = _interleave_residues(o1_ref, buf_ref.at[0], dils[1])
    l1 = _interleave_residues(l1_ref, buf_ref.at[1], dils[1])
    o2 = _interleave_residues(o2_ref, buf_ref.at[2], dils[2])
    l2 = _interleave_residues(l2_ref, buf_ref.at[3], dils[2])
    merged = []
    for sl in range(len(o1)):
        lanes = slice(sl * LANES, (sl + 1) * LANES)
        l0 = l0_ref[:, lanes]
        m = jnp.maximum(jnp.maximum(l0, l1[sl]), l2[sl])
        e0, e1, e2 = jnp.exp(l0 - m), jnp.exp(l1[sl] - m), jnp.exp(l2[sl] - m)
        ob = (e0 * o0_ref[:, lanes].astype(F32) + e1 * o1[sl] + e2 * o2[sl]) / (e0 + e1 + e2)
        merged.append(ob.astype(BF16))
    na = oa_ref.shape[1] // CHUNK
    y = jnp.dot(oa_ref[...], _rows(wo_s, 0, na), preferred_element_type=F32)
    return x + y + jnp.dot(jnp.concatenate(merged, axis=1), _rows(wo_s, na), preferred_element_type=F32)


def _ple_stage(x, g_ref, wg_s, p_ref, wp_s):
    xn = _rms(x, g_ref[...]).astype(BF16)
    gate = jax.nn.sigmoid(jnp.dot(xn, _rows(wg_s), preferred_element_type=F32))
    proj = jnp.dot(p_ref[...].astype(BF16), _rows(wp_s), preferred_element_type=F32)
    return x + gate * proj


def _tail_stages(x, g2_ref, gp_ref, p_ref, gf_ref, ffn_s, wpg_s, wpp_s, a_ref, final):
    x = _ffn_stage(x, g2_ref, *ffn_s, a_ref)
    x = _ple_stage(x, gp_ref, wpg_s, p_ref, wpp_s)
    return _rms(x, gf_ref[...]) if final else x


def _tail_specs(ffn, ple, p, g_final, layer, tm, tile):
    g_ple, w_gate, w_proj = ple
    return (_ffn_specs(ffn, layer)
            + [_layer_spec(g_ple, layer), _chunk_spec(w_gate, layer, 1),
               pl.BlockSpec((None, tm, p.shape[-1]), lambda i: (layer, tile(i), 0)),
               _chunk_spec(w_proj, layer, 1), _resident(g_final.shape, lambda i: (0, 0))])


def _tail_scratch(ffn, ple):
    _, w_gate, w_proj = ple
    return _ffn_scratch(ffn) + [_chunk_scratch(w_gate, 1), _chunk_scratch(w_proj, 1)]


def _tail_staging(ffn, ple, refs, scratch):
    _, w_gate, w_proj = ple
    return _ffn_staging(ffn, refs[:3], scratch[:3]) + [
        (refs[3], scratch[3], 0, _n_chunks(w_gate, 1)), (refs[4], scratch[4], 0, _n_chunks(w_proj, 1))]


def _post_ab_kernel(h_ref, oa_ref, o0_ref, l0_ref, o1_ref, l1_ref, o2_ref, l2_ref, woa_ref, wo_ref,
                    g2_ref, wg_ref, wu_ref, wd_ref, gp_ref, wpg_ref, p_ref, wpp_ref, gf_ref, out_ref,
                    wo_s, wg_s, wu_s, wd_s, wpg_s, wpp_s, a_ref, buf_ref, *, n_stage, staging, final):
    i = pl.program_id(0)

    @pl.when(i < n_stage)
    def _():
        _stage_chunks(i, staging((woa_ref, wo_ref, wg_ref, wu_ref, wd_ref, wpg_ref, wpp_ref),
                                 (wo_s, wo_s, wg_s, wu_s, wd_s, wpg_s, wpp_s)))

    @pl.when(i >= n_stage)
    def _():
        x = _outproj_ab_stage(h_ref[...], oa_ref, o0_ref, l0_ref, o1_ref, l1_ref, o2_ref, l2_ref,
                              wo_s, buf_ref)
        out_ref[...] = _tail_stages(x, g2_ref, gp_ref, p_ref, gf_ref, (wg_s, wu_s, wd_s), wpg_s,
                                    wpp_s, a_ref, final)


def _post_ab(h, oa, obs, lses, w_oa, w_out, ffn, ple, p, g_final, layer, j, bsz, *, final, tm=512):
    t, d = h.shape
    s = t // bsz
    nts = s // tm
    n_oa = _n_chunks(w_oa, 1)
    n_ob = _n_chunks(w_out, 1) - n_oa
    n_stage = _n_chunks(ffn[1], 2)
    tile = lambda i: jnp.maximum(i - n_stage, 0)
    row = lambda i: (tile(i), 0)
    res = lambda i: (tile(i) // nts, 0, tile(i) % nts, 0)
    nb = obs[0].shape[-1]
    in_specs = [pl.BlockSpec((tm, d), row), pl.BlockSpec((tm, oa.shape[1]), row),
                pl.BlockSpec((tm, nb), row), pl.BlockSpec((tm, nb), row)]
    args = [h, oa, obs[0], lses[0]]
    for (_, dil), o, lse in zip(B_PAIRS[1:], obs[1:], lses[1:]):
        in_specs += [pl.BlockSpec((None, dil, tm // dil, nb), res)] * 2
        args += [o, lse]
    wo_b_spec = pl.BlockSpec((None, CHUNK, d), lambda i: (j, n_oa + jnp.minimum(i, n_ob - 1), 0),
                             pipeline_mode=pl.Buffered(1))
    in_specs += [_chunk_spec(w_oa, j, 1), wo_b_spec] + _tail_specs(ffn, ple, p, g_final, layer, tm, tile)
    g_ple, w_gate, w_proj = ple

    def staging(refs, scratch):
        return [(refs[0], scratch[0], 0, n_oa), (refs[1], scratch[1], n_oa, n_ob)] + _tail_staging(
            ffn, ple, refs[2:], scratch[2:])

    return pl.pallas_call(
        functools.partial(_post_ab_kernel, n_stage=n_stage, staging=staging, final=final),
        grid=(n_stage + t // tm,),
        in_specs=in_specs,
        out_specs=pl.BlockSpec((tm, d), row),
        out_shape=jax.ShapeDtypeStruct((t, d), F32),
        scratch_shapes=[_chunk_scratch(w_out, 1)] + _tail_scratch(ffn, ple)
        + [pltpu.VMEM((tm, ffn[1].shape[-1]), BF16), pltpu.VMEM((4, nb // LANES, tm, LANES), F32)],
        compiler_params=_params(1),
        name="post_ab",
    )(*args, w_oa, w_out, *ffn, g_ple, w_gate, p, w_proj, g_final)


def _inproj_c_stage(x, g_ref, w_s, q_ref, k_ref, v_ref):
    xn = _rms(x, g_ref[...]).astype(BF16)
    width = q_ref.shape[1]
    scale = HEAD_DIM ** -0.5
    tn = CHUNK
    for c in range(w_s.shape[0]):
        col = c * tn
        y = jnp.dot(xn, w_s[c], preferred_element_type=F32)
        if col < width:
            q_ref[:, col:col + tn] = (y * scale).astype(BF16)
        elif col < 2 * width:
            k_ref[:, col - width:col - width + tn] = y.astype(BF16)
        else:
            v_ref[:, col - 2 * width:col - 2 * width + tn] = y.astype(BF16)


def _pre_c_kernel(h_ref, g1_ref, wg_ref, wu_ref, wd_ref, gm_ref, w_ref, h1_ref, q_ref, k_ref, v_ref,
                  wg_s, wu_s, wd_s, w_s, a_ref, *, n_stage, staging):
    i = pl.program_id(0)

    @pl.when(i < n_stage)
    def _():
        _stage_chunks(i, staging((wg_ref, wu_ref, wd_ref, w_ref), (wg_s, wu_s, wd_s, w_s)))

    @pl.when(i >= n_stage)
    def _():
        x = _ffn_stage(h_ref[...], g1_ref, wg_s, wu_s, wd_s, a_ref)
        h1_ref[...] = x
        _inproj_c_stage(x, gm_ref, w_s, q_ref, k_ref, v_ref)


def _pre_c(h, ffn, g_mix, w_in, layer, j, *, tm=512):
    t, d = h.shape
    n = w_in.shape[-1]
    n_in = _n_chunks(w_in, 2)
    n_stage = max(n_in, _n_chunks(ffn[1], 2))
    row = lambda i: (jnp.maximum(i - n_stage, 0), 0)

    def staging(refs, scratch):
        return _ffn_staging(ffn, refs[:3], scratch[:3]) + [(refs[3], scratch[3], 0, n_in)]

    return pl.pallas_call(
        functools.partial(_pre_c_kernel, n_stage=n_stage, staging=staging),
        grid=(n_stage + t // tm,),
        in_specs=[pl.BlockSpec((tm, d), row)] + _ffn_specs(ffn, layer)
        + [_layer_spec(g_mix, layer), _chunk_spec(w_in, j, 2)],
        out_specs=[pl.BlockSpec((tm, d), row)] + [pl.BlockSpec((tm, n // 3), row)] * 3,
        out_shape=[jax.ShapeDtypeStruct((t, d), F32)] + [jax.ShapeDtypeStruct((t, n // 3), BF16)] * 3,
        scratch_shapes=_ffn_scratch(ffn) + [_chunk_scratch(w_in, 2),
                                            pltpu.VMEM((tm, ffn[1].shape[-1]), BF16)],
        compiler_params=_params(1),
        name="pre_c",
    )(h, *ffn, g_mix, w_in)


def _na_bias_kernel(rpb_ref, o_ref, *, kh):
    h = pl.program_id(0)
    n_dr, n_dc = rpb_ref.shape[1], rpb_ref.shape[2]
    j = lax.broadcasted_iota(jnp.int32, (GRID_W, LANES), 0)
    lane = lax.broadcasted_iota(jnp.int32, (GRID_W, LANES), 1)
    right = lane >= GRID_W
    c = lane & (GRID_W - 1)
    start = jnp.clip(j - NA_KW // 2, 0, GRID_W - NA_KW)
    inside = (c >= start) & (c < start + NA_KW)
    dc = c - j + (NA_KW - 1)
    pairs = []
    for dr in range(n_dr - 1):
        acc = jnp.full((GRID_W, LANES), NEG_INF, F32)
        for t in range(n_dc):
            acc = jnp.where(dc == t, jnp.where(right, rpb_ref[h, dr + 1, t], rpb_ref[h, dr, t]), acc)
        pairs.append(jnp.where(inside, acc, NEG_INF))
    for off in range(o_ref.shape[0]):
        for a in range(0, kh, 2):
            o_ref[off, :, a * GRID_W:(a + 2) * GRID_W] = pairs[off + a]


def _na_bias(rpb, kh):
    nh = rpb.shape[0]
    return pl.pallas_call(
        functools.partial(_na_bias_kernel, kh=kh),
        grid=(nh,),
        in_specs=[pl.BlockSpec(memory_space=pltpu.SMEM)],
        out_specs=pl.BlockSpec((NA_KH, None, GRID_W, kh * GRID_W), lambda h: (0, h, 0, 0)),
        out_shape=jax.ShapeDtypeStruct((NA_KH, nh, GRID_W, kh * GRID_W), F32),
        compiler_params=_params(1),
        name="na_bias",
    )(rpb)


def _na_kernel(q_ref, k_ref, v_ref, tb_ref, o_ref, *, rows_per_step, rows, kh):
    ig = pl.program_id(2)
    left = lax.broadcasted_iota(jnp.int32, (GRID_W, LANES), 1) < HEAD_DIM
    zero = jnp.zeros((GRID_W, LANES), BF16)

    windows, scores = [], []
    for g in range(rows_per_step):
        i = ig * rows_per_step + g
        rs = jnp.clip(i - kh // 2, 0, rows - kh)
        off = rs - i + (NA_KH - 1)
        win = pl.ds(pl.multiple_of(rs * GRID_W, GRID_W), kh * GRID_W)
        qi = q_ref[g * GRID_W:(g + 1) * GRID_W, :]
        q2 = jnp.concatenate([jnp.where(left, qi, zero), jnp.where(left, zero, qi)], axis=0)
        s = lax.dot_general(q2, k_ref[win, :], (((1,), (1,)), ((), ())), preferred_element_type=F32)
        scores.append(s + tb_ref[off])
        windows.append(win)
    probs, dens = [], []
    for s in scores:
        e = jnp.exp(s - jnp.max(s, axis=-1, keepdims=True))
        dens.append(jnp.sum(e, axis=-1, keepdims=True))
        probs.append(e.astype(BF16))
    for g in range(rows_per_step):
        o2 = jnp.dot(probs[g], v_ref[windows[g], :], preferred_element_type=F32) / dens[g]
        o_ref[g * GRID_W:(g + 1) * GRID_W, :] = jnp.where(
            left, o2[:GRID_W], o2[GRID_W:]).astype(o_ref.dtype)


def _neighbourhood_attention(q, k, v, bias, *, rows_per_step=32):
    bsz, s, width = q.shape
    rows = s // GRID_W
    kh = min(NA_KH, rows)
    n_pairs = width // LANES
    g = rows_per_step
    return pl.pallas_call(
        functools.partial(_na_kernel, rows_per_step=g, rows=rows, kh=kh),
        grid=(bsz, n_pairs, rows // g),
        in_specs=[
            pl.BlockSpec((None, g * GRID_W, LANES), lambda b, p, i: (b, i, p)),
            pl.BlockSpec((None, s, LANES), lambda b, p, i: (b, 0, p)),
            pl.BlockSpec((None, s, LANES), lambda b, p, i: (b, 0, p)),
            pl.BlockSpec((bias.shape[0], None) + bias.shape[2:], lambda b, p, i: (0, p, 0, 0)),
        ],
        out_specs=pl.BlockSpec((None, g * GRID_W, LANES), lambda b, p, i: (b, i, p)),
        out_shape=jax.ShapeDtypeStruct((bsz, s, width), BF16),
        compiler_params=_params(3),
        name="na_attention",
    )(q, k, v, bias)


def _post_c_kernel(h_ref, o_ref, wo_ref, g2_ref, wg_ref, wu_ref, wd_ref, gp_ref, wpg_ref, p_ref,
                   wpp_ref, gf_ref, out_ref, wo_s, wg_s, wu_s, wd_s, wpg_s, wpp_s, a_ref, *,
                   n_stage, staging, final):
    i = pl.program_id(0)

    @pl.when(i < n_stage)
    def _():
        _stage_chunks(i, staging((wo_ref, wg_ref, wu_ref, wd_ref, wpg_ref, wpp_ref),
                                 (wo_s, wg_s, wu_s, wd_s, wpg_s, wpp_s)))

    @pl.when(i >= n_stage)
    def _():
        x = h_ref[...] + jnp.dot(o_ref[...], _rows(wo_s), preferred_element_type=F32)
        out_ref[...] = _tail_stages(x, g2_ref, gp_ref, p_ref, gf_ref, (wg_s, wu_s, wd_s), wpg_s,
                                    wpp_s, a_ref, final)


def _post_c(h, o, w_out, ffn, ple, p, g_final, layer, j, *, final, tm=512):
    t, d = h.shape
    n_stage = _n_chunks(ffn[1], 2)
    tile = lambda i: jnp.maximum(i - n_stage, 0)
    row = lambda i: (tile(i), 0)
    g_ple, w_gate, w_proj = ple

    def staging(refs, scratch):
        return [(refs[0], scratch[0], 0, _n_chunks(w_out, 1))] + _tail_staging(
            ffn, ple, refs[1:], scratch[1:])

    return pl.pallas_call(
        functools.partial(_post_c_kernel, n_stage=n_stage, staging=staging, final=final),
        grid=(n_stage + t // tm,),
        in_specs=[pl.BlockSpec((tm, d), row), pl.BlockSpec((tm, o.shape[1]), row),
                  _chunk_spec(w_out, j, 1)] + _tail_specs(ffn, ple, p, g_final, layer, tm, tile),
        out_specs=pl.BlockSpec((tm, d), row),
        out_shape=jax.ShapeDtypeStruct((t, d), F32),
        scratch_shapes=[_chunk_scratch(w_out, 1)] + _tail_scratch(ffn, ple)
        + [pltpu.VMEM((tm, ffn[1].shape[-1]), BF16)],
        compiler_params=_params(1),
        name="post_c",
    )(h, o, w_out, *ffn, g_ple, w_gate, p, w_proj, g_final)


def _rope_tables(s):
    half = HEAD_DIM // 2
    inv = ROPE_THETA ** (-jnp.arange(half, dtype=F32) / half)
    ang = jnp.arange(s, dtype=jnp.int32).astype(F32)[:, None] * inv[None, :]
    cos, sin = jnp.cos(ang), jnp.sin(ang)
    reps = LANES // HEAD_DIM
    return (jnp.tile(jnp.concatenate([cos, cos], axis=1), (1, reps)),
            jnp.tile(jnp.concatenate([-sin, sin], axis=1), (1, reps)))


def _permute_a_heads(w_in, w_out, sink):
    heads = lambda w, axis: jnp.concatenate(
        [lax.slice_in_dim(w, h * HEAD_DIM, (h + 1) * HEAD_DIM, axis=axis) for h in A_HEAD_ORDER],
        axis=axis)
    return heads(w_in, 2), heads(w_out, 1), sink[:, jnp.array(A_HEAD_ORDER)]


def _ab_attention(qa, ka, va, b0, b1, b2, sink, bsz, s):
    seq = lambda a: a.reshape(bsz, 1, s, a.shape[-1])
    oa = _banded_attention(seq(qa), seq(ka), seq(va), q_col=0, k_col=0, v_col=0, radius=A_RADIUS,
                           qpk=A_GROUP, sink=sink, name="band_a")[0]
    obs, lses = [], []
    for gi, (qkv, (window, dil)) in enumerate(zip((seq(b0), b1, b2), B_PAIRS)):
        o, lse = _banded_attention(qkv, qkv, qkv, q_col=0, k_col=1, v_col=2,
                                   radius=window // (2 * dil), qpk=1, want_lse=True,
                                   name=f"band_b{gi}")
        obs.append(o)
        lses.append(lse)
    flat = lambda a: a.reshape(bsz * s, a.shape[-1])
    obs[0], lses[0] = flat(obs[0]), flat(lses[0])
    return flat(oa), obs, lses


def _c_attention(q, k, v, rpb, bsz, s):
    kh = min(NA_KH, s // GRID_W)
    bias = _na_bias(rpb, kh)
    bias = bias.reshape(NA_KH, bias.shape[1] // 2, 2 * GRID_W, kh * GRID_W)
    o = _neighbourhood_attention(q.reshape(bsz, s, -1), k.reshape(bsz, s, -1),
                                 v.reshape(bsz, s, -1), bias)
    return o.reshape(bsz * s, -1)


def kernel(x, p, norm_ffn1, ffn1_w_gate, ffn1_w_up, ffn1_w_down, norm_mix, w_in_ab, sink_a, w_out_ab, w_in_c, rpb_c, w_out_c, norm_ffn2, ffn2_w_gate, ffn2_w_up, ffn2_w_down, norm_ple, w_ple_gate, w_ple_proj, norm_final):
    bsz, s, d = x.shape
    depth = p.shape[0]
    gains = lambda g: g.reshape(g.shape[0], 1, d)
    ffn1 = (gains(norm_ffn1), ffn1_w_gate, ffn1_w_up, ffn1_w_down)
    ffn2 = (gains(norm_ffn2), ffn2_w_gate, ffn2_w_up, ffn2_w_down)
    ple = (gains(norm_ple), w_ple_gate, w_ple_proj)
    g_mix = gains(norm_mix)
    w_qa, w_oa, sink_a = _permute_a_heads(w_in_ab, w_out_ab, sink_a)
    p2 = p.reshape(depth, bsz * s, p.shape[-1])
    g_final = norm_final.reshape(1, d)
    cos, sin = _rope_tables(s)

    h = x.reshape(bsz * s, d)
    for i in range(depth):
        j = i // 2
        final = i == depth - 1
        if i % 2 == 0:
            h, *proj = _pre_ab(h, ffn1, g_mix, w_qa, w_in_ab, cos, sin, i, j, bsz)
            oa, obs, lses = _ab_attention(*proj, sink_a[j], bsz, s)
            h = _post_ab(h, oa, obs, lses, w_oa, w_out_ab, ffn2, ple, p2, g_final, i, j, bsz,
                         final=final)
        else:
            h, q, k, v = _pre_c(h, ffn1, g_mix, w_in_c, i, j)
            o = _c_attention(q, k, v, rpb_c[j], bsz, s)
            h = _post_c(h, o, w_out_c, ffn2, ple, p2, g_final, i, j, final=final)
    return h.reshape(bsz, s, d)
```

```python
import functools

import jax
import jax.numpy as jnp
from jax import lax
from jax.experimental import pallas as pl
from jax.experimental.pallas import tpu as pltpu

HEAD_DIM = 64
GRID_W = 64
ROPE_THETA = 10000.0
RMS_EPS = 1e-6
A_HEADS = 12
A_KV_HEADS = 4
A_RADIUS = 128
B_PAIRS = ((128, 1), (512, 4), (2048, 16))
B_SLOTS = 4
NA_KH = 8
NA_KW = 16
NEG_INF = -1e30

LANES = 128
ATT_BLOCK = 128
VMEM_LIMIT = 56 * 1024 * 1024
F32 = jnp.float32
BF16 = jnp.bfloat16


def _params(n_axes):
    return pltpu.CompilerParams(dimension_semantics=("arbitrary",) * n_axes,
                                vmem_limit_bytes=VMEM_LIMIT)


def _resident(block_shape, index_map):
    return pl.BlockSpec(block_shape, index_map, pipeline_mode=pl.Buffered(1))


def _rms(x, g):
    ms = jnp.mean(x * x, axis=-1, keepdims=True)
    return x * lax.rsqrt(ms + RMS_EPS) * g


CHUNK = 256
MANY_CHUNKS = 4


def _n_chunks(w, axis):
    return w.shape[axis] // CHUNK


def _chunk_spec(w, idx, axis, block_of=lambda c: c):
    n = _n_chunks(w, axis)
    blk = lambda i: block_of(jnp.minimum(i, n - 1))
    mode = pl.Buffered(2 if n > MANY_CHUNKS else 1)
    if axis == 1:
        return pl.BlockSpec((None, CHUNK, w.shape[2]), lambda i: (idx, blk(i), 0), pipeline_mode=mode)
    return pl.BlockSpec((None, w.shape[1], CHUNK), lambda i: (idx, 0, blk(i)), pipeline_mode=mode)


def _chunk_scratch(w, axis, n=None):
    n = _n_chunks(w, axis) if n is None else n
    shape = (n, CHUNK, w.shape[2]) if axis == 1 else (n, w.shape[1], CHUNK)
    return pltpu.VMEM(shape, BF16)


def _stage_chunks(i, pairs):
    for slab_ref, dst_ref, first, n in pairs:
        dst_ref[first + jnp.minimum(i, n - 1)] = slab_ref[...].astype(BF16)


def _rows(chunks_ref, lo=0, hi=None):
    w = chunks_ref[lo:hi]
    return w.reshape(w.shape[0] * w.shape[1], w.shape[2])


def _ffn_stage(x, g_ref, wg_s, wu_s, wd_s, a_ref):
    xn = _rms(x, g_ref[...]).astype(BF16)
    for f in range(wg_s.shape[0]):
        gate = jnp.dot(xn, wg_s[f], preferred_element_type=F32)
        up = jnp.dot(xn, wu_s[f], preferred_element_type=F32)
        a_ref[:, f * CHUNK:(f + 1) * CHUNK] = (gate * jax.nn.sigmoid(gate) * up).astype(BF16)
    return x + 0.5 * jnp.dot(a_ref[...], _rows(wd_s), preferred_element_type=F32)


def _layer_spec(w, idx):
    tail = w.shape[1:]
    return _resident((None,) + tail, lambda i: (idx,) + (0,) * len(tail))


def _ffn_specs(ffn, layer):
    gain, w_gate, w_up, w_down = ffn
    return [_layer_spec(gain, layer), _chunk_spec(w_gate, layer, 2), _chunk_spec(w_up, layer, 2),
            _chunk_spec(w_down, layer, 1)]


def _ffn_scratch(ffn):
    _, w_gate, w_up, w_down = ffn
    return [_chunk_scratch(w_gate, 2), _chunk_scratch(w_up, 2), _chunk_scratch(w_down, 1)]


def _ffn_staging(ffn, refs, scratch):
    _, w_gate, w_up, w_down = ffn
    ns = (_n_chunks(w_gate, 2), _n_chunks(w_up, 2), _n_chunks(w_down, 1))
    return [(r, s, 0, n) for r, s, n in zip(refs, scratch, ns)]


def _rope(y, cos, sin_signed):
    lane = lax.broadcasted_iota(jnp.int32, y.shape, 1)
    first_half = (lane & (HEAD_DIM // 2)) == 0
    swapped = jnp.where(first_half, pltpu.roll(y, LANES - HEAD_DIM // 2, 1),
                        pltpu.roll(y, HEAD_DIM // 2, 1))
    return y * cos + swapped * sin_signed


A_GROUP = A_HEADS // A_KV_HEADS
A_HEAD_ORDER = (0, 3, 1, 4, 2, 5, 6, 9, 7, 10, 8, 11)
_A_Q, _A_KV = A_HEADS * HEAD_DIM, A_KV_HEADS * HEAD_DIM
_B_W = B_SLOTS * HEAD_DIM


def _scatter_residues(y, buf_ref, out_ref, dil, col):
    n = y.shape[0] // dil
    slabs = y.shape[1] // LANES
    for sl in range(slabs):
        buf_ref[sl] = y[:, sl * LANES:(sl + 1) * LANES]
    for r in range(dil):
        for sl in range(slabs):
            piece = buf_ref[sl, pl.ds(r, n, stride=dil), :]
            out_ref[r, :, col + sl * LANES:col + (sl + 1) * LANES] = piece.astype(out_ref.dtype)


def _inproj_ab_stage(x, g_ref, w_s, cos_ref, sin_ref, qa_ref, ka_ref, va_ref, b_refs, buf_ref):
    xn = _rms(x, g_ref[...]).astype(BF16)
    cos = cos_ref[...]
    sin = sin_ref[...]
    scale = HEAD_DIM ** -0.5
    b_base = _A_Q + 2 * _A_KV
    tn = CHUNK
    for c in range(w_s.shape[0]):
        col = c * tn
        y = jnp.dot(xn, w_s[c], preferred_element_type=F32)
        if col < b_base:
            is_q, is_v = col < _A_Q, col >= _A_Q + _A_KV
        else:
            part = ((col - b_base) % (3 * _B_W)) // _B_W
            is_q, is_v = part == 0, part == 2
        if not is_v:
            y = jnp.concatenate(
                [_rope(y[:, j:j + LANES], cos, sin) for j in range(0, tn, LANES)], axis=1)
        if is_q:
            y = y * scale
        if col < _A_Q:
            qa_ref[:, col:col + tn] = y.astype(BF16)
        elif col < _A_Q + _A_KV:
            ka_ref[:, col - _A_Q:col - _A_Q + tn] = y.astype(BF16)
        elif col < b_base:
            va_ref[:, col - _A_Q - _A_KV:col - _A_Q - _A_KV + tn] = y.astype(BF16)
        else:
            gi, off = divmod(col - b_base, 3 * _B_W)
            dil = B_PAIRS[gi][1]
            if dil == 1:
                b_refs[gi][:, off:off + tn] = y.astype(BF16)
            else:
                _scatter_residues(y, buf_ref, b_refs[gi], dil, off)


def _pre_ab_kernel(h_ref, g1_ref, wg_ref, wu_ref, wd_ref, gm_ref, wqa_ref, w_ref, cos_ref, sin_ref,
                   h1_ref, qa_ref, ka_ref, va_ref, b0_ref, b1_ref, b2_ref,
                   wg_s, wu_s, wd_s, w_s, a_ref, buf_ref, *, n_stage, staging):
    i = pl.program_id(0)

    @pl.when(i < n_stage)
    def _():
        _stage_chunks(i, staging((wg_ref, wu_ref, wd_ref, wqa_ref, w_ref), (wg_s, wu_s, wd_s, w_s, w_s)))

    @pl.when(i >= n_stage)
    def _():
        x = _ffn_stage(h_ref[...], g1_ref, wg_s, wu_s, wd_s, a_ref)
        h1_ref[...] = x
        _inproj_ab_stage(x, gm_ref, w_s, cos_ref, sin_ref, qa_ref, ka_ref, va_ref,
                         (b0_ref, b1_ref, b2_ref), buf_ref)


def _ab_source_block(c):
    first_b = (_A_Q + 2 * _A_KV) // CHUNK
    n_groups = len(B_PAIRS)
    g, part = (c - first_b) // 3, (c - first_b) % 3
    return jnp.where(c < first_b, c, first_b + part * n_groups + g)


def _pre_ab(h, ffn, g_mix, w_qa, w_in, cos, sin, layer, j, bsz, *, tm=512):
    t, d = h.shape
    s = t // bsz
    nts = s // tm
    n_qa = _n_chunks(w_qa, 2)
    n_in = _n_chunks(w_in, 2)
    n_stage = max(n_in, _n_chunks(ffn[1], 2))
    tile = lambda i: jnp.maximum(i - n_stage, 0)
    row = lambda i: (tile(i), 0)
    tab = lambda i: (tile(i) % nts, 0)
    res = lambda i: (tile(i) // nts, 0, tile(i) % nts, 0)
    widths = (d, _A_Q, _A_KV, _A_KV, 3 * _B_W)
    out_specs = [pl.BlockSpec((tm, w), row) for w in widths]
    out_shape = [jax.ShapeDtypeStruct((t, w), F32 if k == 0 else BF16) for k, w in enumerate(widths)]
    for _, dil in B_PAIRS[1:]:
        out_specs.append(pl.BlockSpec((None, dil, tm // dil, 3 * _B_W), res))
        out_shape.append(jax.ShapeDtypeStruct((bsz, dil, s // dil, 3 * _B_W), BF16))

    def staging(refs, scratch):
        return _ffn_staging(ffn, refs[:3], scratch[:3]) + [
            (refs[4], scratch[4], 0, n_in), (refs[3], scratch[3], 0, n_qa)]

    return pl.pallas_call(
        functools.partial(_pre_ab_kernel, n_stage=n_stage, staging=staging),
        grid=(n_stage + t // tm,),
        in_specs=[pl.BlockSpec((tm, d), row)] + _ffn_specs(ffn, layer)
        + [_layer_spec(g_mix, layer), _chunk_spec(w_qa, j, 2),
           _chunk_spec(w_in, j, 2, _ab_source_block),
           pl.BlockSpec((tm, LANES), tab), pl.BlockSpec((tm, LANES), tab)],
        out_specs=out_specs,
        out_shape=out_shape,
        scratch_shapes=_ffn_scratch(ffn) + [_chunk_scratch(w_in, 2),
                                            pltpu.VMEM((tm, ffn[1].shape[-1]), BF16),
                                            pltpu.VMEM((CHUNK // LANES, tm, LANES), F32)],
        compiler_params=_params(1),
        name="pre_ab",
    )(h, *ffn, g_mix, w_qa, w_in, cos, sin)


def _band_kernel(*refs, radius, qpk, length, use_sink, want_lse, tq, n_sub, unroll):
    refs = list(refs)
    sink_ref = refs.pop(0) if use_sink else None
    q_ref, k_ref, v_ref, o_ref = refs[:4]
    lse_ref = refs[4] if want_lse else None
    win = tq + 2 * radius
    n_kg = k_ref.shape[1] // LANES
    nb = 2 * qpk
    t = pl.program_id(2)
    left = lax.broadcasted_iota(jnp.int32, (tq, LANES), 1) < HEAD_DIM
    zero = jnp.zeros((tq, LANES), BF16)
    rel0 = (lax.broadcasted_iota(jnp.int32, (tq, win), 1)
            - lax.broadcasted_iota(jnp.int32, (tq, win), 0))

    def body(jj, carry):
        jobs = []
        for u in range(unroll):
            sub = jj * unroll + u
            rows = pl.ds(pl.multiple_of(sub * tq, tq), tq)
            qpos0 = (t * n_sub + sub) * tq
            kstart = pl.multiple_of(jnp.clip(qpos0 - radius, 0, length - win), radius)
            bias = jnp.where(jnp.abs(rel0 + (kstart - qpos0)) <= radius, 0.0, NEG_INF)
            jobs += [(rows, pl.ds(kstart, win), bias, kg) for kg in range(n_kg)]
        scores = []
        for rows, keys, bias, kg in jobs:
            blocks = []
            for i in range(qpk):
                p = kg * qpk + i
                qp = q_ref[rows, p * LANES:(p + 1) * LANES]
                blocks += [jnp.where(left, qp, zero), jnp.where(left, zero, qp)]
            scores.append(lax.dot_general(
                jnp.concatenate(blocks, axis=0), k_ref[keys, kg * LANES:(kg + 1) * LANES],
                (((1,), (1,)), ((), ())), preferred_element_type=F32))
        stats = []
        for (rows, keys, bias, kg), s in zip(jobs, scores):
            es, dens, ms = [], [], []
            for b in range(nb):
                sb = s[b * tq:(b + 1) * tq] + bias
                mb = jnp.max(sb, axis=-1, keepdims=True)
                if use_sink:
                    sink = sink_ref[kg * nb + b]
                    mb = jnp.maximum(mb, sink)
                eb = jnp.exp(sb - mb)
                db = jnp.sum(eb, axis=-1, keepdims=True)
                if use_sink:
                    db = db + jnp.exp(sink - mb)
                es.append(eb.astype(BF16))
                dens.append(db)
                ms.append(mb)
            stats.append((jnp.concatenate(es, axis=0), dens, ms))
        for (rows, keys, bias, kg), (e, dens, ms) in zip(jobs, stats):
            o2 = jnp.dot(e, v_ref[keys, kg * LANES:(kg + 1) * LANES], preferred_element_type=F32)
            for i in range(qpk):
                p = kg * qpk + i
                even = o2[(2 * i) * tq:(2 * i + 1) * tq] / dens[2 * i]
                odd = o2[(2 * i + 1) * tq:(2 * i + 2) * tq] / dens[2 * i + 1]
                o_ref[rows, p * LANES:(p + 1) * LANES] = jnp.where(left, even, odd).astype(o_ref.dtype)
                if want_lse:
                    lse_e = ms[2 * i] + jnp.log(dens[2 * i])
                    lse_o = ms[2 * i + 1] + jnp.log(dens[2 * i + 1])
                    lse_ref[rows, p * LANES:(p + 1) * LANES] = jnp.where(
                        left, jnp.broadcast_to(lse_e, (tq, LANES)), jnp.broadcast_to(lse_o, (tq, LANES)))
        return carry

    lax.fori_loop(0, n_sub // unroll, body, 0)


def _banded_attention(q, k, v, *, q_col, k_col, v_col, radius, qpk, sink=None, want_lse=False,
                      name):
    bsz, dil, length, _ = q.shape
    tq = ATT_BLOCK
    kw = 2 * LANES
    qw = kw * qpk
    tile = min(length, 1024)
    n_sub = tile // tq
    in_specs = [
        pl.BlockSpec((None, None, tile, qw), lambda b, r, t: (b, r, t, q_col)),
        pl.BlockSpec((None, None, length, kw), lambda b, r, t: (b, r, 0, k_col)),
        pl.BlockSpec((None, None, length, kw), lambda b, r, t: (b, r, 0, v_col)),
    ]
    args = [q, k, v]
    if sink is not None:
        in_specs.insert(0, pl.BlockSpec(memory_space=pltpu.SMEM))
        args.insert(0, sink)
    out_map = lambda b, r, t: (b, r, t, 0)
    out_specs = [pl.BlockSpec((None, None, tile, qw), out_map)]
    out_shape = [jax.ShapeDtypeStruct((bsz, dil, length, qw), BF16)]
    if want_lse:
        out_specs.append(pl.BlockSpec((None, None, tile, qw), out_map))
        out_shape.append(jax.ShapeDtypeStruct((bsz, dil, length, qw), F32))
    return pl.pallas_call(
        functools.partial(_band_kernel, radius=radius, qpk=qpk, length=length,
                          use_sink=sink is not None, want_lse=want_lse, tq=tq, n_sub=n_sub,
                          unroll=min(n_sub, max(2, 4 // qpk))),
        grid=(bsz, dil, length // tile),
        in_specs=in_specs,
        out_specs=out_specs,
        out_shape=out_shape,
        compiler_params=_params(3),
        name=name,
    )(*args)


def _interleave_residues(src_ref, buf_ref, dil):
    n = src_ref.shape[1]
    slabs = src_ref.shape[2] // LANES
    for r in range(dil):
        x = src_ref[r].astype(F32)
        for sl in range(slabs):
            buf_ref[sl, pl.ds(r, n, stride=dil), :] = x[:, sl * LANES:(sl + 1) * LANES]
    return [buf_ref[sl] for sl in range(slabs)]


def _outproj_ab_stage(x, oa_ref, o0_ref, l0_ref, o1_ref, l1_ref, o2_ref, l2_ref, wo_s, buf_ref):
    dils = [dil for _, dil in B_PAIRS]
    o1 = _interleave_residues(o1_ref, buf_ref.at[0], dils[1])
    l1 = _interleave_residues(l1_ref, buf_ref.at[1], dils[1])
    o2 = _interleave_residues(o2_ref, buf_ref.at[2], dils[2])
    l2 = _interleave_residues(l2_ref, buf_ref.at[3], dils[2])
    merged = []
    for sl in range(len(o1)):
        lanes = slice(sl * LANES, (sl + 1) * LANES)
        l0 = l0_ref[:, lanes]
        m = jnp.maximum(jnp.maximum(l0, l1[sl]), l2[sl])
        e0, e1, e2 = jnp.exp(l0 - m), jnp.exp(l1[sl] - m), jnp.exp(l2[sl] - m)
        ob = (e0 * o0_ref[:, lanes].astype(F32) + e1 * o1[sl] + e2 * o2[sl]) / (e0 + e1 + e2)
        merged.append(ob.astype(BF16))
    na = oa_ref.shape[1] // CHUNK
    y = jnp.dot(oa_ref[...], _rows(wo_s, 0, na), preferred_element_type=F32)
    return x + y + jnp.dot(jnp.concatenate(merged, axis=1), _rows(wo_s, na), preferred_element_type=F32)


def _ple_stage(x, g_ref, wg_s, p_ref, wp_s):
    xn = _rms(x, g_ref[...]).astype(BF16)
    gate = jax.nn.sigmoid(jnp.dot(xn, _rows(wg_s), preferred_element_type=F32))
    proj = jnp.dot(p_ref[...].astype(BF16), _rows(wp_s), preferred_element_type=F32)
    return x + gate * proj


def _tail_stages(x, g2_ref, gp_ref, p_ref, gf_ref, ffn_s, wpg_s, wpp_s, a_ref, final):
    x = _ffn_stage(x, g2_ref, *ffn_s, a_ref)
    x = _ple_stage(x, gp_ref, wpg_s, p_ref, wpp_s)
    return _rms(x, gf_ref[...]) if final else x


def _tail_specs(ffn, ple, p, g_final, layer, tm, tile):
    g_ple, w_gate, w_proj = ple
    return (_ffn_specs(ffn, layer)
            + [_layer_spec(g_ple, layer), _chunk_spec(w_gate, layer, 1),
               pl.BlockSpec((None, tm, p.shape[-1]), lambda i: (layer, tile(i), 0)),
               _chunk_spec(w_proj, layer, 1), _resident(g_final.shape, lambda i: (0, 0))])


def _tail_scratch(ffn, ple):
    _, w_gate, w_proj = ple
    return _ffn_scratch(ffn) + [_chunk_scratch(w_gate, 1), _chunk_scratch(w_proj, 1)]


def _tail_staging(ffn, ple, refs, scratch):
    _, w_gate, w_proj = ple
    return _ffn_staging(ffn, refs[:3], scratch[:3]) + [
        (refs[3], scratch[3], 0, _n_chunks(w_gate, 1)), (refs[4], scratch[4], 0, _n_chunks(w_proj, 1))]


def _post_ab_kernel(h_ref, oa_ref, o0_ref, l0_ref, o1_ref, l1_ref, o2_ref, l2_ref, woa_ref, wo_ref,
                    g2_ref, wg_ref, wu_ref, wd_ref, gp_ref, wpg_ref, p_ref, wpp_ref, gf_ref, out_ref,
                    wo_s, wg_s, wu_s, wd_s, wpg_s, wpp_s, a_ref, buf_ref, *, n_stage, staging, final):
    i = pl.program_id(0)

    @pl.when(i < n_stage)
    def _():
        _stage_chunks(i, staging((woa_ref, wo_ref, wg_ref, wu_ref, wd_ref, wpg_ref, wpp_ref),
                                 (wo_s, wo_s, wg_s, wu_s, wd_s, wpg_s, wpp_s)))

    @pl.when(i >= n_stage)
    def _():
        x = _outproj_ab_stage(h_ref[...], oa_ref, o0_ref, l0_ref, o1_ref, l1_ref, o2_ref, l2_ref,
                              wo_s, buf_ref)
        out_ref[...] = _tail_stages(x, g2_ref, gp_ref, p_ref, gf_ref, (wg_s, wu_s, wd_s), wpg_s,
                                    wpp_s, a_ref, final)


def _post_ab(h, oa, obs, lses, w_oa, w_out, ffn, ple, p, g_final, layer, j, bsz, *, final, tm=512):
    t, d = h.shape
    s = t // bsz
    nts = s // tm
    n_oa = _n_chunks(w_oa, 1)
    n_ob = _n_chunks(w_out, 1) - n_oa
    n_stage = _n_chunks(ffn[1], 2)
    tile = lambda i: jnp.maximum(i - n_stage, 0)
    row = lambda i: (tile(i), 0)
    res = lambda i: (tile(i) // nts, 0, tile(i) % nts, 0)
    nb = obs[0].shape[-1]
    in_specs = [pl.BlockSpec((tm, d), row), pl.BlockSpec((tm, oa.shape[1]), row),
                pl.BlockSpec((tm, nb), row), pl.BlockSpec((tm, nb), row)]
    args = [h, oa, obs[0], lses[0]]
    for (_, dil), o, lse in zip(B_PAIRS[1:], obs[1:], lses[1:]):
        in_specs += [pl.BlockSpec((None, dil, tm // dil, nb), res)] * 2
        args += [o, lse]
    wo_b_spec = pl.BlockSpec((None, CHUNK, d), lambda i: (j, n_oa + jnp.minimum(i, n_ob - 1), 0),
                             pipeline_mode=pl.Buffered(1))
    in_specs += [_chunk_spec(w_oa, j, 1), wo_b_spec] + _tail_specs(ffn, ple, p, g_final, layer, tm, tile)
    g_ple, w_gate, w_proj = ple

    def staging(refs, scratch):
        return [(refs[0], scratch[0], 0, n_oa), (refs[1], scratch[1], n_oa, n_ob)] + _tail_staging(
            ffn, ple, refs[2:], scratch[2:])

    return pl.pallas_call(
        functools.partial(_post_ab_kernel, n_stage=n_stage, staging=staging, final=final),
        grid=(n_stage + t // tm,),
        in_specs=in_specs,
        out_specs=pl.BlockSpec((tm, d), row),
        out_shape=jax.ShapeDtypeStruct((t, d), F32),
        scratch_shapes=[_chunk_scratch(w_out, 1)] + _tail_scratch(ffn, ple)
        + [pltpu.VMEM((tm, ffn[1].shape[-1]), BF16), pltpu.VMEM((4, nb // LANES, tm, LANES), F32)],
        compiler_params=_params(1),
        name="post_ab",
    )(*args, w_oa, w_out, *ffn, g_ple, w_gate, p, w_proj, g_final)


def _inproj_c_stage(x, g_ref, w_s, q_ref, k_ref, v_ref):
    xn = _rms(x, g_ref[...]).astype(BF16)
    width = q_ref.shape[1]
    scale = HEAD_DIM ** -0.5
    tn = CHUNK
    for c in range(w_s.shape[0]):
        col = c * tn
        y = jnp.dot(xn, w_s[c], preferred_element_type=F32)
        if col < width:
            q_ref[:, col:col + tn] = (y * scale).astype(BF16)
        elif col < 2 * width:
            k_ref[:, col - width:col - width + tn] = y.astype(BF16)
        else:
            v_ref[:, col - 2 * width:col - 2 * width + tn] = y.astype(BF16)


def _pre_c_kernel(h_ref, g1_ref, wg_ref, wu_ref, wd_ref, gm_ref, w_ref, h1_ref, q_ref, k_ref, v_ref,
                  wg_s, wu_s, wd_s, w_s, a_ref, *, n_stage, staging):
    i = pl.program_id(0)

    @pl.when(i < n_stage)
    def _():
        _stage_chunks(i, staging((wg_ref, wu_ref, wd_ref, w_ref), (wg_s, wu_s, wd_s, w_s)))

    @pl.when(i >= n_stage)
    def _():
        x = _ffn_stage(h_ref[...], g1_ref, wg_s, wu_s, wd_s, a_ref)
        h1_ref[...] = x
        _inproj_c_stage(x, gm_ref, w_s, q_ref, k_ref, v_ref)


def _pre_c(h, ffn, g_mix, w_in, layer, j, *, tm=512):
    t, d = h.shape
    n = w_in.shape[-1]
    n_in = _n_chunks(w_in, 2)
    n_stage = max(n_in, _n_chunks(ffn[1], 2))
    row = lambda i: (jnp.maximum(i - n_stage, 0), 0)

    def staging(refs, scratch):
        return _ffn_staging(ffn, refs[:3], scratch[:3]) + [(refs[3], scratch[3], 0, n_in)]

    return pl.pallas_call(
        functools.partial(_pre_c_kernel, n_stage=n_stage, staging=staging),
        grid=(n_stage + t // tm,),
        in_specs=[pl.BlockSpec((tm, d), row)] + _ffn_specs(ffn, layer)
        + [_layer_spec(g_mix, layer), _chunk_spec(w_in, j, 2)],
        out_specs=[pl.BlockSpec((tm, d), row)] + [pl.BlockSpec((tm, n // 3), row)] * 3,
        out_shape=[jax.ShapeDtypeStruct((t, d), F32)] + [jax.ShapeDtypeStruct((t, n // 3), BF16)] * 3,
        scratch_shapes=_ffn_scratch(ffn) + [_chunk_scratch(w_in, 2),
                                            pltpu.VMEM((tm, ffn[1].shape[-1]), BF16)],
        compiler_params=_params(1),
        name="pre_c",
    )(h, *ffn, g_mix, w_in)


def _na_bias_kernel(rpb_ref, o_ref, *, kh):
    n_dr = rpb_ref.shape[0]
    j = lax.broadcasted_iota(jnp.int32, (GRID_W, LANES), 0)
    lane = lax.broadcasted_iota(jnp.int32, (GRID_W, LANES), 1)
    right = lane >= GRID_W
    c = lane & (GRID_W - 1)
    start = jnp.clip(j - NA_KW // 2, 0, GRID_W - NA_KW)
    inside = (c >= start) & (c < start + NA_KW)
    shift = LANES - (NA_KW - 1)
    skew = lambda dr, extra: pltpu.roll(jnp.broadcast_to(rpb_ref[dr:dr + 1, :], (GRID_W, LANES)),
                                        (shift + extra) % LANES, 1, stride=1, stride_axis=0)
    pairs = [jnp.where(inside, jnp.where(right, skew(dr + 1, GRID_W), skew(dr, 0)), NEG_INF)
             for dr in range(n_dr - 1)]
    for off in range(o_ref.shape[0]):
        for a in range(0, kh, 2):
            o_ref[off, :, a * GRID_W:(a + 2) * GRID_W] = pairs[off + a]


def _na_bias(rpb, kh):
    nh, n_dr, n_dc = rpb.shape
    rows = jnp.pad(rpb, ((0, 0), (0, 0), (0, LANES - n_dc)), constant_values=NEG_INF)
    return pl.pallas_call(
        functools.partial(_na_bias_kernel, kh=kh),
        grid=(nh,),
        in_specs=[pl.BlockSpec((None, n_dr, LANES), lambda h: (h, 0, 0))],
        out_specs=pl.BlockSpec((NA_KH, None, GRID_W, kh * GRID_W), lambda h: (0, h, 0, 0)),
        out_shape=jax.ShapeDtypeStruct((NA_KH, nh, GRID_W, kh * GRID_W), F32),
        compiler_params=_params(1),
        name="na_bias",
    )(rows)


def _na_kernel(q_ref, k_ref, v_ref, tb_ref, o_ref, *, rows_per_step, rows, kh):
    ig = pl.program_id(2)
    left = lax.broadcasted_iota(jnp.int32, (GRID_W, LANES), 1) < HEAD_DIM
    zero = jnp.zeros((GRID_W, LANES), BF16)

    windows, scores = [], []
    for g in range(rows_per_step):
        i = ig * rows_per_step + g
        rs = jnp.clip(i - kh // 2, 0, rows - kh)
        off = rs - i + (NA_KH - 1)
        win = pl.ds(pl.multiple_of(rs * GRID_W, GRID_W), kh * GRID_W)
        qi = q_ref[g * GRID_W:(g + 1) * GRID_W, :]
        q2 = jnp.concatenate([jnp.where(left, qi, zero), jnp.where(left, zero, qi)], axis=0)
        s = lax.dot_general(q2, k_ref[win, :], (((1,), (1,)), ((), ())), preferred_element_type=F32)
        scores.append(s + tb_ref[off])
        windows.append(win)
    probs, dens = [], []
    for s in scores:
        e = jnp.exp(s - jnp.max(s, axis=-1, keepdims=True))
        dens.append(jnp.sum(e, axis=-1, keepdims=True))
        probs.append(e.astype(BF16))
    for g in range(rows_per_step):
        o2 = jnp.dot(probs[g], v_ref[windows[g], :], preferred_element_type=F32) / dens[g]
        o_ref[g * GRID_W:(g + 1) * GRID_W, :] = jnp.where(
            left, o2[:GRID_W], o2[GRID_W:]).astype(o_ref.dtype)


def _neighbourhood_attention(q, k, v, bias, *, rows_per_step=32):
    bsz, s, width = q.shape
    rows = s // GRID_W
    kh = min(NA_KH, rows)
    n_pairs = width // LANES
    g = rows_per_step
    return pl.pallas_call(
        functools.partial(_na_kernel, rows_per_step=g, rows=rows, kh=kh),
        grid=(bsz, n_pairs, rows // g),
        in_specs=[
            pl.BlockSpec((None, g * GRID_W, LANES), lambda b, p, i: (b, i, p)),
            pl.BlockSpec((None, s, LANES), lambda b, p, i: (b, 0, p)),
            pl.BlockSpec((None, s, LANES), lambda b, p, i: (b, 0, p)),
            pl.BlockSpec((bias.shape[0], None) + bias.shape[2:], lambda b, p, i: (0, p, 0, 0)),
        ],
        out_specs=pl.BlockSpec((None, g * GRID_W, LANES), lambda b, p, i: (b, i, p)),
        out_shape=jax.ShapeDtypeStruct((bsz, s, width), BF16),
        compiler_params=_params(3),
        name="na_attention",
    )(q, k, v, bias)


def _post_c_kernel(h_ref, o_ref, wo_ref, g2_ref, wg_ref, wu_ref, wd_ref, gp_ref, wpg_ref, p_ref,
                   wpp_ref, gf_ref, out_ref, wo_s, wg_s, wu_s, wd_s, wpg_s, wpp_s, a_ref, *,
                   n_stage, staging, final):
    i = pl.program_id(0)

    @pl.when(i < n_stage)
    def _():
        _stage_chunks(i, staging((wo_ref, wg_ref, wu_ref, wd_ref, wpg_ref, wpp_ref),
                                 (wo_s, wg_s, wu_s, wd_s, wpg_s, wpp_s)))

    @pl.when(i >= n_stage)
    def _():
        x = h_ref[...] + jnp.dot(o_ref[...], _rows(wo_s), preferred_element_type=F32)
        out_ref[...] = _tail_stages(x, g2_ref, gp_ref, p_ref, gf_ref, (wg_s, wu_s, wd_s), wpg_s,
                                    wpp_s, a_ref, final)


def _post_c(h, o, w_out, ffn, ple, p, g_final, layer, j, *, final, tm=512):
    t, d = h.shape
    n_stage = _n_chunks(ffn[1], 2)
    tile = lambda i: jnp.maximum(i - n_stage, 0)
    row = lambda i: (tile(i), 0)
    g_ple, w_gate, w_proj = ple

    def staging(refs, scratch):
        return [(refs[0], scratch[0], 0, _n_chunks(w_out, 1))] + _tail_staging(
            ffn, ple, refs[1:], scratch[1:])

    return pl.pallas_call(
        functools.partial(_post_c_kernel, n_stage=n_stage, staging=staging, final=final),
        grid=(n_stage + t // tm,),
        in_specs=[pl.BlockSpec((tm, d), row), pl.BlockSpec((tm, o.shape[1]), row),
                  _chunk_spec(w_out, j, 1)] + _tail_specs(ffn, ple, p, g_final, layer, tm, tile),
        out_specs=pl.BlockSpec((tm, d), row),
        out_shape=jax.ShapeDtypeStruct((t, d), F32),
        scratch_shapes=[_chunk_scratch(w_out, 1)] + _tail_scratch(ffn, ple)
        + [pltpu.VMEM((tm, ffn[1].shape[-1]), BF16)],
        compiler_params=_params(1),
        name="post_c",
    )(h, o, w_out, *ffn, g_ple, w_gate, p, w_proj, g_final)


def _rope_tables(s):
    half = HEAD_DIM // 2
    inv = ROPE_THETA ** (-jnp.arange(half, dtype=F32) / half)
    ang = jnp.arange(s, dtype=jnp.int32).astype(F32)[:, None] * inv[None, :]
    cos, sin = jnp.cos(ang), jnp.sin(ang)
    reps = LANES // HEAD_DIM
    return (jnp.tile(jnp.concatenate([cos, cos], axis=1), (1, reps)),
            jnp.tile(jnp.concatenate([-sin, sin], axis=1), (1, reps)))


def _permute_a_heads(w_in, w_out, sink):
    heads = lambda w, axis: jnp.concatenate(
        [lax.slice_in_dim(w, h * HEAD_DIM, (h + 1) * HEAD_DIM, axis=axis) for h in A_HEAD_ORDER],
        axis=axis)
    return heads(w_in, 2), heads(w_out, 1), sink[:, jnp.array(A_HEAD_ORDER)]


def _ab_attention(qa, ka, va, b0, b1, b2, sink, bsz, s):
    seq = lambda a: a.reshape(bsz, 1, s, a.shape[-1])
    oa = _banded_attention(seq(qa), seq(ka), seq(va), q_col=0, k_col=0, v_col=0, radius=A_RADIUS,
                           qpk=A_GROUP, sink=sink, name="band_a")[0]
    obs, lses = [], []
    for gi, (qkv, (window, dil)) in enumerate(zip((seq(b0), b1, b2), B_PAIRS)):
        o, lse = _banded_attention(qkv, qkv, qkv, q_col=0, k_col=1, v_col=2,
                                   radius=window // (2 * dil), qpk=1, want_lse=True,
                                   name=f"band_b{gi}")
        obs.append(o)
        lses.append(lse)
    flat = lambda a: a.reshape(bsz * s, a.shape[-1])
    obs[0], lses[0] = flat(obs[0]), flat(lses[0])
    return flat(oa), obs, lses


def _c_attention(q, k, v, rpb, bsz, s):
    kh = min(NA_KH, s // GRID_W)
    bias = _na_bias(rpb, kh)
    bias = bias.reshape(NA_KH, bias.shape[1] // 2, 2 * GRID_W, kh * GRID_W)
    o = _neighbourhood_attention(q.reshape(bsz, s, -1), k.reshape(bsz, s, -1),
                                 v.reshape(bsz, s, -1), bias)
    return o.reshape(bsz * s, -1)


def kernel(x, p, norm_ffn1, ffn1_w_gate, ffn1_w_up, ffn1_w_down, norm_mix, w_in_ab, sink_a, w_out_ab, w_in_c, rpb_c, w_out_c, norm_ffn2, ffn2_w_gate, ffn2_w_up, ffn2_w_down, norm_ple, w_ple_gate, w_ple_proj, norm_final):
    bsz, s, d = x.shape
    depth = p.shape[0]
    gains = lambda g: g.reshape(g.shape[0], 1, d)
    ffn1 = (gains(norm_ffn1), ffn1_w_gate, ffn1_w_up, ffn1_w_down)
    ffn2 = (gains(norm_ffn2), ffn2_w_gate, ffn2_w_up, ffn2_w_down)
    ple = (gains(norm_ple), w_ple_gate, w_ple_proj)
    g_mix = gains(norm_mix)
    w_qa, w_oa, sink_a = _permute_a_heads(w_in_ab, w_out_ab, sink_a)
    p2 = p.reshape(depth, bsz * s, p.shape[-1])
    g_final = norm_final.reshape(1, d)
    cos, sin = _rope_tables(s)

    h = x.reshape(bsz * s, d)
    for i in range(depth):
        j = i // 2
        final = i == depth - 1
        if i % 2 == 0:
            h, *proj = _pre_ab(h, ffn1, g_mix, w_qa, w_in_ab, cos, sin, i, j, bsz)
            oa, obs, lses = _ab_attention(*proj, sink_a[j], bsz, s)
            h = _post_ab(h, oa, obs, lses, w_oa, w_out_ab, ffn2, ple, p2, g_final, i, j, bsz,
                         final=final)
        else:
            h, q, k, v = _pre_c(h, ffn1, g_mix, w_in_c, i, j)
            o = _c_attention(q, k, v, rpb_c[j], bsz, s)
            h = _post_c(h, o, w_out_c, ffn2, ple, p2, g_final, i, j, final=final)
    return h.reshape(bsz, s, d)
```

```python
import functools

import jax
import jax.numpy as jnp
from jax import lax
from jax.experimental import pallas as pl
from jax.experimental.pallas import tpu as pltpu

HEAD_DIM = 64
GRID_W = 64
ROPE_THETA = 10000.0
RMS_EPS = 1e-6
A_HEADS = 12
A_KV_HEADS = 4
A_RADIUS = 128
B_PAIRS = ((128, 1), (512, 4), (2048, 16))
B_SLOTS = 4
NA_KH = 8
NA_KW = 16
NEG_INF = -1e30

LANES = 128
ATT_BLOCK = 128
VMEM_LIMIT = 56 * 1024 * 1024
F32 = jnp.float32
BF16 = jnp.bfloat16


def _params(n_axes):
    return pltpu.CompilerParams(dimension_semantics=("arbitrary",) * n_axes,
                                vmem_limit_bytes=VMEM_LIMIT)


def _resident(block_shape, index_map):
    return pl.BlockSpec(block_shape, index_map, pipeline_mode=pl.Buffered(1))


def _rms(x, g):
    ms = jnp.mean(x * x, axis=-1, keepdims=True)
    return x * lax.rsqrt(ms + RMS_EPS) * g


CHUNK = 256
MANY_CHUNKS = 4


def _n_chunks(w, axis):
    return w.shape[axis] // CHUNK


def _chunk_spec(w, idx, axis, block_of=lambda c: c):
    n = _n_chunks(w, axis)
    blk = lambda i: block_of(jnp.minimum(i, n - 1))
    mode = pl.Buffered(2 if n > MANY_CHUNKS else 1)
    if axis == 1:
        return pl.BlockSpec((None, CHUNK, w.shape[2]), lambda i: (idx, blk(i), 0), pipeline_mode=mode)
    return pl.BlockSpec((None, w.shape[1], CHUNK), lambda i: (idx, 0, blk(i)), pipeline_mode=mode)


def _chunk_scratch(w, axis, n=None):
    n = _n_chunks(w, axis) if n is None else n
    shape = (n, CHUNK, w.shape[2]) if axis == 1 else (n, w.shape[1], CHUNK)
    return pltpu.VMEM(shape, BF16)


def _stage_chunks(i, pairs):
    for slab_ref, dst_ref, first, n in pairs:
        dst_ref[first + jnp.minimum(i, n - 1)] = slab_ref[...].astype(BF16)


def _rows(chunks_ref, lo=0, hi=None):
    w = chunks_ref[lo:hi]
    return w.reshape(w.shape[0] * w.shape[1], w.shape[2])


def _ffn_stage(x, g_ref, wg_s, wu_s, wd_s, a_ref):
    xn = _rms(x, g_ref[...]).astype(BF16)
    for f in range(wg_s.shape[0]):
        gate = jnp.dot(xn, wg_s[f], preferred_element_type=F32)
        up = jnp.dot(xn, wu_s[f], preferred_element_type=F32)
        a_ref[:, f * CHUNK:(f + 1) * CHUNK] = (gate * jax.nn.sigmoid(gate) * up).astype(BF16)
    return x + 0.5 * jnp.dot(a_ref[...], _rows(wd_s), preferred_element_type=F32)


def _layer_spec(w, idx):
    tail = w.shape[1:]
    return _resident((None,) + tail, lambda i: (idx,) + (0,) * len(tail))


def _ffn_specs(ffn, layer):
    gain, w_gate, w_up, w_down = ffn
    return [_layer_spec(gain, layer), _chunk_spec(w_gate, layer, 2), _chunk_spec(w_up, layer, 2),
            _chunk_spec(w_down, layer, 1)]


def _ffn_scratch(ffn):
    _, w_gate, w_up, w_down = ffn
    return [_chunk_scratch(w_gate, 2), _chunk_scratch(w_up, 2), _chunk_scratch(w_down, 1)]


def _ffn_staging(ffn, refs, scratch):
    _, w_gate, w_up, w_down = ffn
    ns = (_n_chunks(w_gate, 2), _n_chunks(w_up, 2), _n_chunks(w_down, 1))
    return [(r, s, 0, n) for r, s, n in zip(refs, scratch, ns)]


def _rope(y, cos, sin_signed):
    lane = lax.broadcasted_iota(jnp.int32, y.shape, 1)
    first_half = (lane & (HEAD_DIM // 2)) == 0
    swapped = jnp.where(first_half, pltpu.roll(y, LANES - HEAD_DIM // 2, 1),
                        pltpu.roll(y, HEAD_DIM // 2, 1))
    return y * cos + swapped * sin_signed


A_GROUP = A_HEADS // A_KV_HEADS
A_HEAD_ORDER = (0, 3, 1, 4, 2, 5, 6, 9, 7, 10, 8, 11)
_A_Q, _A_KV = A_HEADS * HEAD_DIM, A_KV_HEADS * HEAD_DIM
_B_W = B_SLOTS * HEAD_DIM


def _scatter_residues(y, buf_ref, out_ref, dil, col):
    n = y.shape[0] // dil
    slabs = y.shape[1] // LANES
    for sl in range(slabs):
        buf_ref[sl] = y[:, sl * LANES:(sl + 1) * LANES]
    for r in range(dil):
        for sl in range(slabs):
            piece = buf_ref[sl, pl.ds(r, n, stride=dil), :]
            out_ref[r, :, col + sl * LANES:col + (sl + 1) * LANES] = piece.astype(out_ref.dtype)


def _inproj_ab_stage(x, g_ref, w_s, cos_ref, sin_ref, qa_ref, ka_ref, va_ref, b_refs, buf_ref):
    xn = _rms(x, g_ref[...]).astype(BF16)
    cos = cos_ref[...]
    sin = sin_ref[...]
    scale = HEAD_DIM ** -0.5
    b_base = _A_Q + 2 * _A_KV
    tn = CHUNK
    for c in range(w_s.shape[0]):
        col = c * tn
        y = jnp.dot(xn, w_s[c], preferred_element_type=F32)
        if col < b_base:
            is_q, is_v = col < _A_Q, col >= _A_Q + _A_KV
        else:
            part = ((col - b_base) % (3 * _B_W)) // _B_W
            is_q, is_v = part == 0, part == 2
        if not is_v:
            y = jnp.concatenate(
                [_rope(y[:, j:j + LANES], cos, sin) for j in range(0, tn, LANES)], axis=1)
        if is_q:
            y = y * scale
        if col < _A_Q:
            qa_ref[:, col:col + tn] = y.astype(BF16)
        elif col < _A_Q + _A_KV:
            ka_ref[:, col - _A_Q:col - _A_Q + tn] = y.astype(BF16)
        elif col < b_base:
            va_ref[:, col - _A_Q - _A_KV:col - _A_Q - _A_KV + tn] = y.astype(BF16)
        else:
            gi, off = divmod(col - b_base, 3 * _B_W)
            dil = B_PAIRS[gi][1]
            if dil == 1:
                b_refs[gi][:, off:off + tn] = y.astype(BF16)
            else:
                _scatter_residues(y, buf_ref.at[c % 2], b_refs[gi], dil, off)


def _pre_ab_kernel(h_ref, g1_ref, wg_ref, wu_ref, wd_ref, gm_ref, wqa_ref, w_ref, cos_ref, sin_ref,
                   h1_ref, qa_ref, ka_ref, va_ref, b0_ref, b1_ref, b2_ref,
                   wg_s, wu_s, wd_s, w_s, a_ref, buf_ref, *, n_stage, staging):
    i = pl.program_id(0)

    @pl.when(i < n_stage)
    def _():
        _stage_chunks(i, staging((wg_ref, wu_ref, wd_ref, wqa_ref, w_ref), (wg_s, wu_s, wd_s, w_s, w_s)))

    @pl.when(i >= n_stage)
    def _():
        x = _ffn_stage(h_ref[...], g1_ref, wg_s, wu_s, wd_s, a_ref)
        h1_ref[...] = x
        _inproj_ab_stage(x, gm_ref, w_s, cos_ref, sin_ref, qa_ref, ka_ref, va_ref,
                         (b0_ref, b1_ref, b2_ref), buf_ref)


def _ab_source_block(c):
    first_b = (_A_Q + 2 * _A_KV) // CHUNK
    n_groups = len(B_PAIRS)
    g, part = (c - first_b) // 3, (c - first_b) % 3
    return jnp.where(c < first_b, c, first_b + part * n_groups + g)


def _pre_ab(h, ffn, g_mix, w_qa, w_in, cos, sin, layer, j, bsz, *, tm=512):
    t, d = h.shape
    s = t // bsz
    nts = s // tm
    n_qa = _n_chunks(w_qa, 2)
    n_in = _n_chunks(w_in, 2)
    n_stage = max(n_in, _n_chunks(ffn[1], 2))
    tile = lambda i: jnp.maximum(i - n_stage, 0)
    row = lambda i: (tile(i), 0)
    tab = lambda i: (tile(i) % nts, 0)
    res = lambda i: (tile(i) // nts, 0, tile(i) % nts, 0)
    widths = (d, _A_Q, _A_KV, _A_KV, 3 * _B_W)
    out_specs = [pl.BlockSpec((tm, w), row) for w in widths]
    out_shape = [jax.ShapeDtypeStruct((t, w), F32 if k == 0 else BF16) for k, w in enumerate(widths)]
    for _, dil in B_PAIRS[1:]:
        out_specs.append(pl.BlockSpec((None, dil, tm // dil, 3 * _B_W), res))
        out_shape.append(jax.ShapeDtypeStruct((bsz, dil, s // dil, 3 * _B_W), BF16))

    def staging(refs, scratch):
        return _ffn_staging(ffn, refs[:3], scratch[:3]) + [
            (refs[4], scratch[4], 0, n_in), (refs[3], scratch[3], 0, n_qa)]

    return pl.pallas_call(
        functools.partial(_pre_ab_kernel, n_stage=n_stage, staging=staging),
        grid=(n_stage + t // tm,),
        in_specs=[pl.BlockSpec((tm, d), row)] + _ffn_specs(ffn, layer)
        + [_layer_spec(g_mix, layer), _chunk_spec(w_qa, j, 2),
           _chunk_spec(w_in, j, 2, _ab_source_block),
           pl.BlockSpec((tm, LANES), tab), pl.BlockSpec((tm, LANES), tab)],
        out_specs=out_specs,
        out_shape=out_shape,
        scratch_shapes=_ffn_scratch(ffn) + [_chunk_scratch(w_in, 2),
                                            pltpu.VMEM((tm, ffn[1].shape[-1]), BF16),
                                            pltpu.VMEM((2, CHUNK // LANES, tm, LANES), F32)],
        compiler_params=_params(1),
        name="pre_ab",
    )(h, *ffn, g_mix, w_qa, w_in, cos, sin)


def _band_kernel(*refs, radius, qpk, length, use_sink, want_lse, tq, n_sub, unroll):
    refs = list(refs)
    sink_ref = refs.pop(0) if use_sink else None
    q_ref, k_ref, v_ref, o_ref = refs[:4]
    lse_ref = refs[4] if want_lse else None
    win = tq + 2 * radius
    n_kg = k_ref.shape[1] // LANES
    nb = 2 * qpk
    t = pl.program_id(2)
    left = lax.broadcasted_iota(jnp.int32, (tq, LANES), 1) < HEAD_DIM
    zero = jnp.zeros((tq, LANES), BF16)
    rel0 = (lax.broadcasted_iota(jnp.int32, (tq, win), 1)
            - lax.broadcasted_iota(jnp.int32, (tq, win), 0))

    def body(jj, carry):
        jobs = []
        for u in range(unroll):
            sub = jj * unroll + u
            rows = pl.ds(pl.multiple_of(sub * tq, tq), tq)
            qpos0 = (t * n_sub + sub) * tq
            kstart = pl.multiple_of(jnp.clip(qpos0 - radius, 0, length - win), radius)
            bias = jnp.where(jnp.abs(rel0 + (kstart - qpos0)) <= radius, 0.0, NEG_INF)
            jobs += [(rows, pl.ds(kstart, win), bias, kg) for kg in range(n_kg)]
        scores = []
        for rows, keys, bias, kg in jobs:
            blocks = []
            for i in range(qpk):
                p = kg * qpk + i
                qp = q_ref[rows, p * LANES:(p + 1) * LANES]
                blocks += [jnp.where(left, qp, zero), jnp.where(left, zero, qp)]
            scores.append(lax.dot_general(
                jnp.concatenate(blocks, axis=0), k_ref[keys, kg * LANES:(kg + 1) * LANES],
                (((1,), (1,)), ((), ())), preferred_element_type=F32))
        stats = []
        for (rows, keys, bias, kg), s in zip(jobs, scores):
            es, dens, ms = [], [], []
            for b in range(nb):
                sb = s[b * tq:(b + 1) * tq] + bias
                mb = jnp.max(sb, axis=-1, keepdims=True)
                if use_sink:
                    sink = sink_ref[kg * nb + b]
                    mb = jnp.maximum(mb, sink)
                eb = jnp.exp(sb - mb)
                db = jnp.sum(eb, axis=-1, keepdims=True)
                if use_sink:
                    db = db + jnp.exp(sink - mb)
                es.append(eb.astype(BF16))
                dens.append(db)
                ms.append(mb)
            stats.append((jnp.concatenate(es, axis=0), dens, ms))
        for (rows, keys, bias, kg), (e, dens, ms) in zip(jobs, stats):
            o2 = jnp.dot(e, v_ref[keys, kg * LANES:(kg + 1) * LANES], preferred_element_type=F32)
            for i in range(qpk):
                p = kg * qpk + i
                even = o2[(2 * i) * tq:(2 * i + 1) * tq] / dens[2 * i]
                odd = o2[(2 * i + 1) * tq:(2 * i + 2) * tq] / dens[2 * i + 1]
                o_ref[rows, p * LANES:(p + 1) * LANES] = jnp.where(left, even, odd).astype(o_ref.dtype)
                if want_lse:
                    lse_e = ms[2 * i] + jnp.log(dens[2 * i])
                    lse_o = ms[2 * i + 1] + jnp.log(dens[2 * i + 1])
                    lse_ref[rows, p * LANES:(p + 1) * LANES] = jnp.where(
                        left, jnp.broadcast_to(lse_e, (tq, LANES)), jnp.broadcast_to(lse_o, (tq, LANES)))
        return carry

    lax.fori_loop(0, n_sub // unroll, body, 0)


def _banded_attention(q, k, v, *, q_col, k_col, v_col, radius, qpk, sink=None, want_lse=False,
                      name):
    bsz, dil, length, _ = q.shape
    tq = ATT_BLOCK
    kw = 2 * LANES
    qw = kw * qpk
    tile = min(length, 1024)
    n_sub = tile // tq
    in_specs = [
        pl.BlockSpec((None, None, tile, qw), lambda b, r, t: (b, r, t, q_col)),
        pl.BlockSpec((None, None, length, kw), lambda b, r, t: (b, r, 0, k_col)),
        pl.BlockSpec((None, None, length, kw), lambda b, r, t: (b, r, 0, v_col)),
    ]
    args = [q, k, v]
    if sink is not None:
        in_specs.insert(0, pl.BlockSpec(memory_space=pltpu.SMEM))
        args.insert(0, sink)
    out_map = lambda b, r, t: (b, r, t, 0)
    out_specs = [pl.BlockSpec((None, None, tile, qw), out_map)]
    out_shape = [jax.ShapeDtypeStruct((bsz, dil, length, qw), BF16)]
    if want_lse:
        out_specs.append(pl.BlockSpec((None, None, tile, qw), out_map))
        out_shape.append(jax.ShapeDtypeStruct((bsz, dil, length, qw), F32))
    return pl.pallas_call(
        functools.partial(_band_kernel, radius=radius, qpk=qpk, length=length,
                          use_sink=sink is not None, want_lse=want_lse, tq=tq, n_sub=n_sub,
                          unroll=min(n_sub, max(2, 4 // qpk))),
        grid=(bsz, dil, length // tile),
        in_specs=in_specs,
        out_specs=out_specs,
        out_shape=out_shape,
        compiler_params=_params(3),
        name=name,
    )(*args)


def _interleave_residues(src_ref, buf_ref, dil):
    n = src_ref.shape[1]
    slabs = src_ref.shape[2] // LANES
    for r in range(dil):
        x = src_ref[r].astype(F32)
        for sl in range(slabs):
            buf_ref[sl, pl.ds(r, n, stride=dil), :] = x[:, sl * LANES:(sl + 1) * LANES]
    return [buf_ref[sl] for sl in range(slabs)]


def _outproj_ab_stage(x, oa_ref, o0_ref, l0_ref, o1_ref, l1_ref, o2_ref, l2_ref, wo_s, buf_ref):
    dils = [dil for _, dil in B_PAIRS]
    o1 = _interleave_residues(o1_ref, buf_ref.at[0], dils[1])
    l1 = _interleave_residues(l1_ref, buf_ref.at[1], dils[1])
    o2 = _interleave_residues(o2_ref, buf_ref.at[2], dils[2])
    l2 = _interleave_residues(l2_ref, buf_ref.at[3], dils[2])
    merged = []
    for sl in range(len(o1)):
        lanes = slice(sl * LANES, (sl + 1) * LANES)
        l0 = l0_ref[:, lanes]
        m = jnp.maximum(jnp.maximum(l0, l1[sl]), l2[sl])
        e0, e1, e2 = jnp.exp(l0 - m), jnp.exp(l1[sl] - m), jnp.exp(l2[sl] - m)
        ob = (e0 * o0_ref[:, lanes].astype(F32) + e1 * o1[sl] + e2 * o2[sl]) / (e0 + e1 + e2)
        merged.append(ob.astype(BF16))
    na = oa_ref.shape[1] // CHUNK
    y = jnp.dot(oa_ref[...], _rows(wo_s, 0, na), preferred_element_type=F32)
    return x + y + jnp.dot(jnp.concatenate(merged, axis=1), _rows(wo_s, na), preferred_element_type=F32)


def _ple_stage(x, g_ref, wg_s, p_ref, wp_s):
    xn = _rms(x, g_ref[...]).astype(BF16)
    gate = jax.nn.sigmoid(jnp.dot(xn, _rows(wg_s), preferred_element_type=F32))
    proj = jnp.dot(p_ref[...].astype(BF16), _rows(wp_s), preferred_element_type=F32)
    return x + gate * proj


def _tail_stages(x, g2_ref, gp_ref, p_ref, gf_ref, ffn_s, wpg_s, wpp_s, a_ref, final):
    x = _ffn_stage(x, g2_ref, *ffn_s, a_ref)
    x = _ple_stage(x, gp_ref, wpg_s, p_ref, wpp_s)
    return _rms(x, gf_ref[...]) if final else x


def _tail_specs(ffn, ple, p, g_final, layer, tm, tile):
    g_ple, w_gate, w_proj = ple
    return (_ffn_specs(ffn, layer)
            + [_layer_spec(g_ple, layer), _chunk_spec(w_gate, layer, 1),
               pl.BlockSpec((None, tm, p.shape[-1]), lambda i: (layer, tile(i), 0)),
               _chunk_spec(w_proj, layer, 1), _resident(g_final.shape, lambda i: (0, 0))])


def _tail_scratch(ffn, ple):
    _, w_gate, w_proj = ple
    return _ffn_scratch(ffn) + [_chunk_scratch(w_gate, 1), _chunk_scratch(w_proj, 1)]


def _tail_staging(ffn, ple, refs, scratch):
    _, w_gate, w_proj = ple
    return _ffn_staging(ffn, refs[:3], scratch[:3]) + [
        (refs[3], scratch[3], 0, _n_chunks(w_gate, 1)), (refs[4], scratch[4], 0, _n_chunks(w_proj, 1))]


def _post_ab_kernel(h_ref, oa_ref, o0_ref, l0_ref, o1_ref, l1_ref, o2_ref, l2_ref, woa_ref, wo_ref,
                    g2_ref, wg_ref, wu_ref, wd_ref, gp_ref, wpg_ref, p_ref, wpp_ref, gf_ref, out_ref,
                    wo_s, wg_s, wu_s, wd_s, wpg_s, wpp_s, a_ref, buf_ref, *, n_stage, staging, final):
    i = pl.program_id(0)

    @pl.when(i < n_stage)
    def _():
        _stage_chunks(i, staging((woa_ref, wo_ref, wg_ref, wu_ref, wd_ref, wpg_ref, wpp_ref),
                                 (wo_s, wo_s, wg_s, wu_s, wd_s, wpg_s, wpp_s)))

    @pl.when(i >= n_stage)
    def _():
        x = _outproj_ab_stage(h_ref[...], oa_ref, o0_ref, l0_ref, o1_ref, l1_ref, o2_ref, l2_ref,
                              wo_s, buf_ref)
        out_ref[...] = _tail_stages(x, g2_ref, gp_ref, p_ref, gf_ref, (wg_s, wu_s, wd_s), wpg_s,
                                    wpp_s, a_ref, final)


def _post_ab(h, oa, obs, lses, w_oa, w_out, ffn, ple, p, g_final, layer, j, bsz, *, final, tm=512):
    t, d = h.shape
    s = t // bsz
    nts = s // tm
    n_oa = _n_chunks(w_oa, 1)
    n_ob = _n_chunks(w_out, 1) - n_oa
    n_stage = _n_chunks(ffn[1], 2)
    tile = lambda i: jnp.maximum(i - n_stage, 0)
    row = lambda i: (tile(i), 0)
    res = lambda i: (tile(i) // nts, 0, tile(i) % nts, 0)
    nb = obs[0].shape[-1]
    in_specs = [pl.BlockSpec((tm, d), row), pl.BlockSpec((tm, oa.shape[1]), row),
                pl.BlockSpec((tm, nb), row), pl.BlockSpec((tm, nb), row)]
    args = [h, oa, obs[0], lses[0]]
    for (_, dil), o, lse in zip(B_PAIRS[1:], obs[1:], lses[1:]):
        in_specs += [pl.BlockSpec((None, dil, tm // dil, nb), res)] * 2
        args += [o, lse]
    wo_b_spec = pl.BlockSpec((None, CHUNK, d), lambda i: (j, n_oa + jnp.minimum(i, n_ob - 1), 0),
                             pipeline_mode=pl.Buffered(1))
    in_specs += [_chunk_spec(w_oa, j, 1), wo_b_spec] + _tail_specs(ffn, ple, p, g_final, layer, tm, tile)
    g_ple, w_gate, w_proj = ple

    def staging(refs, scratch):
        return [(refs[0], scratch[0], 0, n_oa), (refs[1], scratch[1], n_oa, n_ob)] + _tail_staging(
            ffn, ple, refs[2:], scratch[2:])

    return pl.pallas_call(
        functools.partial(_post_ab_kernel, n_stage=n_stage, staging=staging, final=final),
        grid=(n_stage + t // tm,),
        in_specs=in_specs,
        out_specs=pl.BlockSpec((tm, d), row),
        out_shape=jax.ShapeDtypeStruct((t, d), F32),
        scratch_shapes=[_chunk_scratch(w_out, 1)] + _tail_scratch(ffn, ple)
        + [pltpu.VMEM((tm, ffn[1].shape[-1]), BF16), pltpu.VMEM((4, nb // LANES, tm, LANES), F32)],
        compiler_params=_params(1),
        name="post_ab",
    )(*args, w_oa, w_out, *ffn, g_ple, w_gate, p, w_proj, g_final)


def _inproj_c_stage(x, g_ref, w_s, q_ref, k_ref, v_ref):
    xn = _rms(x, g_ref[...]).astype(BF16)
    width = q_ref.shape[1]
    scale = HEAD_DIM ** -0.5
    tn = CHUNK
    for c in range(w_s.shape[0]):
        col = c * tn
        y = jnp.dot(xn, w_s[c], preferred_element_type=F32)
        if col < width:
            q_ref[:, col:col + tn] = (y * scale).astype(BF16)
        elif col < 2 * width:
            k_ref[:, col - width:col - width + tn] = y.astype(BF16)
        else:
            v_ref[:, col - 2 * width:col - 2 * width + tn] = y.astype(BF16)


def _pre_c_kernel(h_ref, g1_ref, wg_ref, wu_ref, wd_ref, gm_ref, w_ref, h1_ref, q_ref, k_ref, v_ref,
                  wg_s, wu_s, wd_s, w_s, a_ref, *, n_stage, staging):
    i = pl.program_id(0)

    @pl.when(i < n_stage)
    def _():
        _stage_chunks(i, staging((wg_ref, wu_ref, wd_ref, w_ref), (wg_s, wu_s, wd_s, w_s)))

    @pl.when(i >= n_stage)
    def _():
        x = _ffn_stage(h_ref[...], g1_ref, wg_s, wu_s, wd_s, a_ref)
        h1_ref[...] = x
        _inproj_c_stage(x, gm_ref, w_s, q_ref, k_ref, v_ref)


def _pre_c(h, ffn, g_mix, w_in, layer, j, *, tm=512):
    t, d = h.shape
    n = w_in.shape[-1]
    n_in = _n_chunks(w_in, 2)
    n_stage = max(n_in, _n_chunks(ffn[1], 2))
    row = lambda i: (jnp.maximum(i - n_stage, 0), 0)

    def staging(refs, scratch):
        return _ffn_staging(ffn, refs[:3], scratch[:3]) + [(refs[3], scratch[3], 0, n_in)]

    return pl.pallas_call(
        functools.partial(_pre_c_kernel, n_stage=n_stage, staging=staging),
        grid=(n_stage + t // tm,),
        in_specs=[pl.BlockSpec((tm, d), row)] + _ffn_specs(ffn, layer)
        + [_layer_spec(g_mix, layer), _chunk_spec(w_in, j, 2)],
        out_specs=[pl.BlockSpec((tm, d), row)] + [pl.BlockSpec((tm, n // 3), row)] * 3,
        out_shape=[jax.ShapeDtypeStruct((t, d), F32)] + [jax.ShapeDtypeStruct((t, n // 3), BF16)] * 3,
        scratch_shapes=_ffn_scratch(ffn) + [_chunk_scratch(w_in, 2),
                                            pltpu.VMEM((tm, ffn[1].shape[-1]), BF16)],
        compiler_params=_params(1),
        name="pre_c",
    )(h, *ffn, g_mix, w_in)


BIAS_HEADS = 4


def _na_bias_kernel(rpb_ref, o_ref, *, kh):
    n_heads, n_dr = rpb_ref.shape[:2]
    j = lax.broadcasted_iota(jnp.int32, (GRID_W, LANES), 0)
    lane = lax.broadcasted_iota(jnp.int32, (GRID_W, LANES), 1)
    right = lane >= GRID_W
    c = lane & (GRID_W - 1)
    start = jnp.clip(j - NA_KW // 2, 0, GRID_W - NA_KW)
    inside = (c >= start) & (c < start + NA_KW)
    shift = LANES - (NA_KW - 1)
    for h in range(n_heads):
        skew = lambda dr, extra: pltpu.roll(
            jnp.broadcast_to(rpb_ref[h, dr:dr + 1, :], (GRID_W, LANES)),
            (shift + extra) % LANES, 1, stride=1, stride_axis=0)
        pairs = [jnp.where(inside, jnp.where(right, skew(dr + 1, GRID_W), skew(dr, 0)), NEG_INF)
                 for dr in range(n_dr - 1)]
        for off in range(o_ref.shape[0]):
            for a in range(0, kh, 2):
                o_ref[off, h, :, a * GRID_W:(a + 2) * GRID_W] = pairs[off + a]


def _na_bias(rpb, kh):
    nh, n_dr, n_dc = rpb.shape
    rows = jnp.pad(rpb, ((0, 0), (0, 0), (0, LANES - n_dc)), constant_values=NEG_INF)
    return pl.pallas_call(
        functools.partial(_na_bias_kernel, kh=kh),
        grid=(nh // BIAS_HEADS,),
        in_specs=[pl.BlockSpec((BIAS_HEADS, n_dr, LANES), lambda h: (h, 0, 0))],
        out_specs=pl.BlockSpec((NA_KH, BIAS_HEADS, GRID_W, kh * GRID_W), lambda h: (0, h, 0, 0)),
        out_shape=jax.ShapeDtypeStruct((NA_KH, nh, GRID_W, kh * GRID_W), F32),
        compiler_params=_params(1),
        name="na_bias",
    )(rows)


def _na_kernel(q_ref, k_ref, v_ref, tb_ref, o_ref, *, rows_per_step, rows, kh):
    ig = pl.program_id(2)
    left = lax.broadcasted_iota(jnp.int32, (GRID_W, LANES), 1) < HEAD_DIM
    zero = jnp.zeros((GRID_W, LANES), BF16)

    windows, scores = [], []
    for g in range(rows_per_step):
        i = ig * rows_per_step + g
        rs = jnp.clip(i - kh // 2, 0, rows - kh)
        off = rs - i + (NA_KH - 1)
        win = pl.ds(pl.multiple_of(rs * GRID_W, GRID_W), kh * GRID_W)
        qi = q_ref[g * GRID_W:(g + 1) * GRID_W, :]
        q2 = jnp.concatenate([jnp.where(left, qi, zero), jnp.where(left, zero, qi)], axis=0)
        s = lax.dot_general(q2, k_ref[win, :], (((1,), (1,)), ((), ())), preferred_element_type=F32)
        scores.append(s + tb_ref[off])
        windows.append(win)
    probs, dens = [], []
    for s in scores:
        e = jnp.exp(s - jnp.max(s, axis=-1, keepdims=True))
        dens.append(jnp.sum(e, axis=-1, keepdims=True))
        probs.append(e.astype(BF16))
    for g in range(rows_per_step):
        o2 = jnp.dot(probs[g], v_ref[windows[g], :], preferred_element_type=F32) / dens[g]
        o_ref[g * GRID_W:(g + 1) * GRID_W, :] = jnp.where(
            left, o2[:GRID_W], o2[GRID_W:]).astype(o_ref.dtype)


def _neighbourhood_attention(q, k, v, bias, *, rows_per_step=64):
    bsz, s, width = q.shape
    rows = s // GRID_W
    kh = min(NA_KH, rows)
    n_pairs = width // LANES
    g = rows_per_step
    return pl.pallas_call(
        functools.partial(_na_kernel, rows_per_step=g, rows=rows, kh=kh),
        grid=(bsz, n_pairs, rows // g),
        in_specs=[
            pl.BlockSpec((None, g * GRID_W, LANES), lambda b, p, i: (b, i, p)),
            pl.BlockSpec((None, s, LANES), lambda b, p, i: (b, 0, p)),
            pl.BlockSpec((None, s, LANES), lambda b, p, i: (b, 0, p)),
            pl.BlockSpec((bias.shape[0], None) + bias.shape[2:], lambda b, p, i: (0, p, 0, 0)),
        ],
        out_specs=pl.BlockSpec((None, g * GRID_W, LANES), lambda b, p, i: (b, i, p)),
        out_shape=jax.ShapeDtypeStruct((bsz, s, width), BF16),
        compiler_params=_params(3),
        name="na_attention",
    )(q, k, v, bias)


def _post_c_kernel(h_ref, o_ref, wo_ref, g2_ref, wg_ref, wu_ref, wd_ref, gp_ref, wpg_ref, p_ref,
                   wpp_ref, gf_ref, out_ref, wo_s, wg_s, wu_s, wd_s, wpg_s, wpp_s, a_ref, *,
                   n_stage, staging, final):
    i = pl.program_id(0)

    @pl.when(i < n_stage)
    def _():
        _stage_chunks(i, staging((wo_ref, wg_ref, wu_ref, wd_ref, wpg_ref, wpp_ref),
                                 (wo_s, wg_s, wu_s, wd_s, wpg_s, wpp_s)))

    @pl.when(i >= n_stage)
    def _():
        x = h_ref[...] + jnp.dot(o_ref[...], _rows(wo_s), preferred_element_type=F32)
        out_ref[...] = _tail_stages(x, g2_ref, gp_ref, p_ref, gf_ref, (wg_s, wu_s, wd_s), wpg_s,
                                    wpp_s, a_ref, final)


def _post_c(h, o, w_out, ffn, ple, p, g_final, layer, j, *, final, tm=512):
    t, d = h.shape
    n_stage = _n_chunks(ffn[1], 2)
    tile = lambda i: jnp.maximum(i - n_stage, 0)
    row = lambda i: (tile(i), 0)
    g_ple, w_gate, w_proj = ple

    def staging(refs, scratch):
        return [(refs[0], scratch[0], 0, _n_chunks(w_out, 1))] + _tail_staging(
            ffn, ple, refs[1:], scratch[1:])

    return pl.pallas_call(
        functools.partial(_post_c_kernel, n_stage=n_stage, staging=staging, final=final),
        grid=(n_stage + t // tm,),
        in_specs=[pl.BlockSpec((tm, d), row), pl.BlockSpec((tm, o.shape[1]), row),
                  _chunk_spec(w_out, j, 1)] + _tail_specs(ffn, ple, p, g_final, layer, tm, tile),
        out_specs=pl.BlockSpec((tm, d), row),
        out_shape=jax.ShapeDtypeStruct((t, d), F32),
        scratch_shapes=[_chunk_scratch(w_out, 1)] + _tail_scratch(ffn, ple)
        + [pltpu.VMEM((tm, ffn[1].shape[-1]), BF16)],
        compiler_params=_params(1),
        name="post_c",
    )(h, o, w_out, *ffn, g_ple, w_gate, p, w_proj, g_final)


def _rope_tables(s):
    half = HEAD_DIM // 2
    inv = ROPE_THETA ** (-jnp.arange(half, dtype=F32) / half)
    ang = jnp.arange(s, dtype=jnp.int32).astype(F32)[:, None] * inv[None, :]
    cos, sin = jnp.cos(ang), jnp.sin(ang)
    reps = LANES // HEAD_DIM
    return (jnp.tile(jnp.concatenate([cos, cos], axis=1), (1, reps)),
            jnp.tile(jnp.concatenate([-sin, sin], axis=1), (1, reps)))


def _permute_a_heads(w_in, w_out, sink):
    heads = lambda w, axis: jnp.concatenate(
        [lax.slice_in_dim(w, h * HEAD_DIM, (h + 1) * HEAD_DIM, axis=axis) for h in A_HEAD_ORDER],
        axis=axis)
    return heads(w_in, 2), heads(w_out, 1), sink[:, jnp.array(A_HEAD_ORDER)]


def _ab_attention(qa, ka, va, b0, b1, b2, sink, bsz, s):
    seq = lambda a: a.reshape(bsz, 1, s, a.shape[-1])
    oa = _banded_attention(seq(qa), seq(ka), seq(va), q_col=0, k_col=0, v_col=0, radius=A_RADIUS,
                           qpk=A_GROUP, sink=sink, name="band_a")[0]
    obs, lses = [], []
    for gi, (qkv, (window, dil)) in enumerate(zip((seq(b0), b1, b2), B_PAIRS)):
        o, lse = _banded_attention(qkv, qkv, qkv, q_col=0, k_col=1, v_col=2,
                                   radius=window // (2 * dil), qpk=1, want_lse=True,
                                   name=f"band_b{gi}")
        obs.append(o)
        lses.append(lse)
    flat = lambda a: a.reshape(bsz * s, a.shape[-1])
    obs[0], lses[0] = flat(obs[0]), flat(lses[0])
    return flat(oa), obs, lses


def _c_attention(q, k, v, rpb, bsz, s):
    kh = min(NA_KH, s // GRID_W)
    bias = _na_bias(rpb, kh)
    bias = bias.reshape(NA_KH, bias.shape[1] // 2, 2 * GRID_W, kh * GRID_W)
    o = _neighbourhood_attention(q.reshape(bsz, s, -1), k.reshape(bsz, s, -1),
                                 v.reshape(bsz, s, -1), bias)
    return o.reshape(bsz * s, -1)


def kernel(x, p, norm_ffn1, ffn1_w_gate, ffn1_w_up, ffn1_w_down, norm_mix, w_in_ab, sink_a, w_out_ab, w_in_c, rpb_c, w_out_c, norm_ffn2, ffn2_w_gate, ffn2_w_up, ffn2_w_down, norm_ple, w_ple_gate, w_ple_proj, norm_final):
    bsz, s, d = x.shape
    depth = p.shape[0]
    gains = lambda g: g.reshape(g.shape[0], 1, d)
    ffn1 = (gains(norm_ffn1), ffn1_w_gate, ffn1_w_up, ffn1_w_down)
    ffn2 = (gains(norm_ffn2), ffn2_w_gate, ffn2_w_up, ffn2_w_down)
    ple = (gains(norm_ple), w_ple_gate, w_ple_proj)
    g_mix = gains(norm_mix)
    w_qa, w_oa, sink_a = _permute_a_heads(w_in_ab, w_out_ab, sink_a)
    p2 = p.reshape(depth, bsz * s, p.shape[-1])
    g_final = norm_final.reshape(1, d)
    cos, sin = _rope_tables(s)

    h = x.reshape(bsz * s, d)
    for i in range(depth):
        j = i // 2
        final = i == depth - 1
        if i % 2 == 0:
            h, *proj = _pre_ab(h, ffn1, g_mix, w_qa, w_in_ab, cos, sin, i, j, bsz)
            oa, obs, lses = _ab_attention(*proj, sink_a[j], bsz, s)
            h = _post_ab(h, oa, obs, lses, w_oa, w_out_ab, ffn2, ple, p2, g_final, i, j, bsz,
                         final=final)
        else:
            h, q, k, v = _pre_c(h, ffn1, g_mix, w_in_c, i, j)
            o = _c_attention(q, k, v, rpb_c[j], bsz, s)
            h = _post_c(h, o, w_out_c, ffn2, ple, p2, g_final, i, j, final=final)
    return h.reshape(bsz, s, d)
```

```python
import functools

import jax
import jax.numpy as jnp
from jax import lax
from jax.experimental import pallas as pl
from jax.experimental.pallas import tpu as pltpu

HEAD_DIM = 64
GRID_W = 64
ROPE_THETA = 10000.0
RMS_EPS = 1e-6
A_HEADS = 12
A_KV_HEADS = 4
A_RADIUS = 128
B_PAIRS = ((128, 1), (512, 4), (2048, 16))
B_SLOTS = 4
NA_KH = 8
NA_KW = 16
NEG_INF = -1e30
LOG2E = 1.4426950408889634
LN2 = 0.6931471805599453

LANES = 128
ATT_BLOCK = 128
VMEM_LIMIT = 56 * 1024 * 1024
F32 = jnp.float32
BF16 = jnp.bfloat16


def _params(n_axes):
    return pltpu.CompilerParams(dimension_semantics=("arbitrary",) * n_axes,
                                vmem_limit_bytes=VMEM_LIMIT)


def _resident(block_shape, index_map):
    return pl.BlockSpec(block_shape, index_map, pipeline_mode=pl.Buffered(1))


def _rms(x, g):
    ms = jnp.mean(x * x, axis=-1, keepdims=True)
    return x * lax.rsqrt(ms + RMS_EPS) * g


CHUNK = 256
MANY_CHUNKS = 4


def _n_chunks(w, axis):
    return w.shape[axis] // CHUNK


def _chunk_spec(w, idx, axis, block_of=lambda c: c):
    n = _n_chunks(w, axis)
    blk = lambda i: block_of(jnp.minimum(i, n - 1))
    mode = pl.Buffered(2 if n > MANY_CHUNKS else 1)
    if axis == 1:
        return pl.BlockSpec((None, CHUNK, w.shape[2]), lambda i: (idx, blk(i), 0), pipeline_mode=mode)
    return pl.BlockSpec((None, w.shape[1], CHUNK), lambda i: (idx, 0, blk(i)), pipeline_mode=mode)


def _chunk_scratch(w, axis, n=None):
    n = _n_chunks(w, axis) if n is None else n
    shape = (n, CHUNK, w.shape[2]) if axis == 1 else (n, w.shape[1], CHUNK)
    return pltpu.VMEM(shape, BF16)


def _stage_chunks(i, pairs):
    for slab_ref, dst_ref, first, n in pairs:
        dst_ref[first + jnp.minimum(i, n - 1)] = slab_ref[...].astype(BF16)


def _rows(chunks_ref, lo=0, hi=None):
    w = chunks_ref[lo:hi]
    return w.reshape(w.shape[0] * w.shape[1], w.shape[2])


def _ffn_stage(x, g_ref, wg_s, wu_s, wd_s, a_ref):
    xn = _rms(x, g_ref[...]).astype(BF16)
    for f in range(wg_s.shape[0]):
        gate = jnp.dot(xn, wg_s[f], preferred_element_type=F32)
        up = jnp.dot(xn, wu_s[f], preferred_element_type=F32)
        a_ref[:, f * CHUNK:(f + 1) * CHUNK] = (gate * jax.nn.sigmoid(gate) * up).astype(BF16)
    return x + 0.5 * jnp.dot(a_ref[...], _rows(wd_s), preferred_element_type=F32)


def _layer_spec(w, idx):
    tail = w.shape[1:]
    return _resident((None,) + tail, lambda i: (idx,) + (0,) * len(tail))


def _ffn_specs(ffn, layer):
    gain, w_gate, w_up, w_down = ffn
    return [_layer_spec(gain, layer), _chunk_spec(w_gate, layer, 2), _chunk_spec(w_up, layer, 2),
            _chunk_spec(w_down, layer, 1)]


def _ffn_scratch(ffn):
    _, w_gate, w_up, w_down = ffn
    return [_chunk_scratch(w_gate, 2), _chunk_scratch(w_up, 2), _chunk_scratch(w_down, 1)]


def _ffn_staging(ffn, refs, scratch):
    _, w_gate, w_up, w_down = ffn
    ns = (_n_chunks(w_gate, 2), _n_chunks(w_up, 2), _n_chunks(w_down, 1))
    return [(r, s, 0, n) for r, s, n in zip(refs, scratch, ns)]


def _rope(y, cos, sin_signed):
    lane = lax.broadcasted_iota(jnp.int32, y.shape, 1)
    first_half = (lane & (HEAD_DIM // 2)) == 0
    swapped = jnp.where(first_half, pltpu.roll(y, LANES - HEAD_DIM // 2, 1),
                        pltpu.roll(y, HEAD_DIM // 2, 1))
    return y * cos + swapped * sin_signed


A_GROUP = A_HEADS // A_KV_HEADS
A_HEAD_ORDER = (0, 3, 1, 4, 2, 5, 6, 9, 7, 10, 8, 11)
_A_Q, _A_KV = A_HEADS * HEAD_DIM, A_KV_HEADS * HEAD_DIM
_B_W = B_SLOTS * HEAD_DIM


def _scatter_residues(y, buf_ref, out_ref, dil, col):
    n = y.shape[0] // dil
    slabs = y.shape[1] // LANES
    for sl in range(slabs):
        buf_ref[sl] = y[:, sl * LANES:(sl + 1) * LANES]
    for r in range(dil):
        for sl in range(slabs):
            piece = buf_ref[sl, pl.ds(r, n, stride=dil), :]
            out_ref[r, :, col + sl * LANES:col + (sl + 1) * LANES] = piece.astype(out_ref.dtype)


def _inproj_ab_stage(x, g_ref, w_s, tab_ref, ab_ref, b_refs, buf_ref):
    xn = _rms(x, g_ref[...]).astype(BF16)
    cos = tab_ref[:, :LANES]
    sin = tab_ref[:, LANES:]
    scale = HEAD_DIM ** -0.5 * LOG2E
    b_base = _A_Q + 2 * _A_KV
    tn = CHUNK
    for c in range(w_s.shape[0]):
        col = c * tn
        y = jnp.dot(xn, w_s[c], preferred_element_type=F32)
        if col < b_base:
            is_q, is_v = col < _A_Q, col >= _A_Q + _A_KV
        else:
            part = ((col - b_base) % (3 * _B_W)) // _B_W
            is_q, is_v = part == 0, part == 2
        if not is_v:
            y = jnp.concatenate(
                [_rope(y[:, j:j + LANES], cos, sin) for j in range(0, tn, LANES)], axis=1)
        if is_q:
            y = y * scale
        gi, off = divmod(max(col - b_base, 0), 3 * _B_W)
        dil = B_PAIRS[gi][1]
        if dil == 1:
            ab_ref[:, col:col + tn] = y.astype(BF16)
        else:
            _scatter_residues(y, buf_ref.at[c % 2], b_refs[gi - 1], dil, off)


def _pre_ab_kernel(h_ref, g1_ref, wg_ref, wu_ref, wd_ref, gm_ref, wqa_ref, w_ref, tab_ref,
                   h1_ref, ab_ref, b1_ref, b2_ref,
                   wg_s, wu_s, wd_s, w_s, a_ref, buf_ref, *, n_stage, staging):
    i = pl.program_id(0)

    @pl.when(i < n_stage)
    def _():
        _stage_chunks(i, staging((wg_ref, wu_ref, wd_ref, wqa_ref, w_ref), (wg_s, wu_s, wd_s, w_s, w_s)))

    @pl.when(i >= n_stage)
    def _():
        x = _ffn_stage(h_ref[...], g1_ref, wg_s, wu_s, wd_s, a_ref)
        h1_ref[...] = x
        _inproj_ab_stage(x, gm_ref, w_s, tab_ref, ab_ref, (b1_ref, b2_ref), buf_ref)


def _ab_source_block(c):
    first_b = (_A_Q + 2 * _A_KV) // CHUNK
    n_groups = len(B_PAIRS)
    g, part = (c - first_b) // 3, (c - first_b) % 3
    return jnp.where(c < first_b, c, first_b + part * n_groups + g)


def _pre_ab(h, ffn, g_mix, w_qa, w_in, rope_tab, layer, j, bsz, *, tm=512):
    t, d = h.shape
    s = t // bsz
    nts = s // tm
    n_qa = _n_chunks(w_qa, 2)
    n_in = _n_chunks(w_in, 2)
    n_stage = max(n_in, _n_chunks(ffn[1], 2))
    tile = lambda i: jnp.maximum(i - n_stage, 0)
    row = lambda i: (tile(i), 0)
    tab = lambda i: (tile(i) % nts, 0)
    res = lambda i: (tile(i) // nts, 0, tile(i) % nts, 0)
    n_ab = _A_Q + 2 * _A_KV + 3 * _B_W
    out_specs = [pl.BlockSpec((tm, d), row), pl.BlockSpec((tm, n_ab), row)]
    out_shape = [jax.ShapeDtypeStruct((t, d), F32), jax.ShapeDtypeStruct((t, n_ab), BF16)]
    for _, dil in B_PAIRS[1:]:
        out_specs.append(pl.BlockSpec((None, dil, tm // dil, 3 * _B_W), res))
        out_shape.append(jax.ShapeDtypeStruct((bsz, dil, s // dil, 3 * _B_W), BF16))

    def staging(refs, scratch):
        return _ffn_staging(ffn, refs[:3], scratch[:3]) + [
            (refs[4], scratch[4], 0, n_in), (refs[3], scratch[3], 0, n_qa)]

    return pl.pallas_call(
        functools.partial(_pre_ab_kernel, n_stage=n_stage, staging=staging),
        grid=(n_stage + t // tm,),
        in_specs=[pl.BlockSpec((tm, d), row)] + _ffn_specs(ffn, layer)
        + [_layer_spec(g_mix, layer), _chunk_spec(w_qa, j, 2),
           _chunk_spec(w_in, j, 2, _ab_source_block), pl.BlockSpec((tm, 2 * LANES), tab)],
        out_specs=out_specs,
        out_shape=out_shape,
        scratch_shapes=_ffn_scratch(ffn) + [_chunk_scratch(w_in, 2),
                                            pltpu.VMEM((tm, ffn[1].shape[-1]), BF16),
                                            pltpu.VMEM((2, CHUNK // LANES, tm, LANES), F32)],
        compiler_params=_params(1),
        name="pre_ab",
    )(h, *ffn, g_mix, w_qa, w_in, rope_tab)


def _band_kernel(*refs, radius, qpk, length, use_sink, want_lse, tq, n_sub, unroll):
    refs = list(refs)
    sink_ref = refs.pop(0) if use_sink else None
    q_ref, k_ref, v_ref, o_ref = refs[:4]
    lse_ref = refs[4] if want_lse else None
    win = tq + 2 * radius
    n_kg = k_ref.shape[1] // LANES
    nb = 2 * qpk
    t = pl.program_id(2)
    left = lax.broadcasted_iota(jnp.int32, (tq, LANES), 1) < HEAD_DIM
    zero = jnp.zeros((tq, LANES), BF16)
    rel0 = (lax.broadcasted_iota(jnp.int32, (tq, win), 1)
            - lax.broadcasted_iota(jnp.int32, (tq, win), 0))

    def body(jj, carry):
        jobs = []
        for u in range(unroll):
            sub = jj * unroll + u
            rows = pl.ds(pl.multiple_of(sub * tq, tq), tq)
            qpos0 = (t * n_sub + sub) * tq
            kstart = pl.multiple_of(jnp.clip(qpos0 - radius, 0, length - win), radius)
            bias = jnp.where(jnp.abs(rel0 + (kstart - qpos0)) <= radius, 0.0, NEG_INF)
            jobs += [(rows, pl.ds(kstart, win), bias, kg) for kg in range(n_kg)]
        scores = []
        for rows, keys, bias, kg in jobs:
            blocks = []
            for i in range(qpk):
                p = kg * qpk + i
                qp = q_ref[rows, p * LANES:(p + 1) * LANES]
                blocks += [jnp.where(left, qp, zero), jnp.where(left, zero, qp)]
            scores.append(lax.dot_general(
                jnp.concatenate(blocks, axis=0), k_ref[keys, kg * LANES:(kg + 1) * LANES],
                (((1,), (1,)), ((), ())), preferred_element_type=F32))
        stats = []
        for (rows, keys, bias, kg), s in zip(jobs, scores):
            es, dens, ms = [], [], []
            for b in range(nb):
                sb = s[b * tq:(b + 1) * tq] + bias
                mb = jnp.max(sb, axis=-1, keepdims=True)
                if use_sink:
                    sink = sink_ref[kg * nb + b] * LOG2E
                    mb = jnp.maximum(mb, sink)
                eb = jnp.exp2(sb - mb)
                db = jnp.sum(eb, axis=-1, keepdims=True)
                if use_sink:
                    db = db + jnp.exp2(sink - mb)
                es.append(eb.astype(BF16))
                dens.append(db)
                ms.append(mb)
            stats.append((jnp.concatenate(es, axis=0), dens, ms))
        for (rows, keys, bias, kg), (e, dens, ms) in zip(jobs, stats):
            o2 = jnp.dot(e, v_ref[keys, kg * LANES:(kg + 1) * LANES], preferred_element_type=F32)
            for i in range(qpk):
                p = kg * qpk + i
                even = o2[(2 * i) * tq:(2 * i + 1) * tq] / dens[2 * i]
                odd = o2[(2 * i + 1) * tq:(2 * i + 2) * tq] / dens[2 * i + 1]
                o_ref[rows, p * LANES:(p + 1) * LANES] = jnp.where(left, even, odd).astype(o_ref.dtype)
                if want_lse:
                    lse_e = ms[2 * i] * LN2 + jnp.log(dens[2 * i])
                    lse_o = ms[2 * i + 1] * LN2 + jnp.log(dens[2 * i + 1])
                    lse_ref[rows, p * LANES:(p + 1) * LANES] = jnp.where(
                        left, jnp.broadcast_to(lse_e, (tq, LANES)), jnp.broadcast_to(lse_o, (tq, LANES)))
        return carry

    lax.fori_loop(0, n_sub // unroll, body, 0)


def _banded_attention(q, k, v, *, q_col, k_col, v_col, radius, qpk, sink=None, want_lse=False,
                      name):
    bsz, dil, length, _ = q.shape
    tq = ATT_BLOCK
    kw = 2 * LANES
    qw = kw * qpk
    tile = min(length, 1024)
    n_sub = tile // tq
    in_specs = [
        pl.BlockSpec((None, None, tile, qw), lambda b, r, t: (b, r, t, q_col)),
        pl.BlockSpec((None, None, length, kw), lambda b, r, t: (b, r, 0, k_col)),
        pl.BlockSpec((None, None, length, kw), lambda b, r, t: (b, r, 0, v_col)),
    ]
    args = [q, k, v]
    if sink is not None:
        in_specs.insert(0, pl.BlockSpec(memory_space=pltpu.SMEM))
        args.insert(0, sink)
    out_map = lambda b, r, t: (b, r, t, 0)
    out_specs = [pl.BlockSpec((None, None, tile, qw), out_map)]
    out_shape = [jax.ShapeDtypeStruct((bsz, dil, length, qw), BF16)]
    if want_lse:
        out_specs.append(pl.BlockSpec((None, None, tile, qw), out_map))
        out_shape.append(jax.ShapeDtypeStruct((bsz, dil, length, qw), F32))
    return pl.pallas_call(
        functools.partial(_band_kernel, radius=radius, qpk=qpk, length=length,
                          use_sink=sink is not None, want_lse=want_lse, tq=tq, n_sub=n_sub,
                          unroll=min(n_sub, max(2, 4 // qpk))),
        grid=(bsz, dil, length // tile),
        in_specs=in_specs,
        out_specs=out_specs,
        out_shape=out_shape,
        compiler_params=_params(3),
        name=name,
    )(*args)


def _interleave_residues(src_ref, buf_ref, dil):
    n = src_ref.shape[1]
    slabs = src_ref.shape[2] // LANES
    for r in range(dil):
        x = src_ref[r].astype(F32)
        for sl in range(slabs):
            buf_ref[sl, pl.ds(r, n, stride=dil), :] = x[:, sl * LANES:(sl + 1) * LANES]
    return [buf_ref[sl] for sl in range(slabs)]


def _outproj_ab_stage(x, oa_ref, o0_ref, l0_ref, o1_ref, l1_ref, o2_ref, l2_ref, wo_s, buf_ref):
    dils = [dil for _, dil in B_PAIRS]
    o1 = _interleave_residues(o1_ref, buf_ref.at[0], dils[1])
    l1 = _interleave_residues(l1_ref, buf_ref.at[1], dils[1])
    o2 = _interleave_residues(o2_ref, buf_ref.at[2], dils[2])
    l2 = _interleave_residues(l2_ref, buf_ref.at[3], dils[2])
    merged = []
    for sl in range(len(o1)):
        lanes = slice(sl * LANES, (sl + 1) * LANES)
        l0 = l0_ref[:, lanes]
        m = jnp.maximum(jnp.maximum(l0, l1[sl]), l2[sl])
        e0, e1, e2 = jnp.exp(l0 - m), jnp.exp(l1[sl] - m), jnp.exp(l2[sl] - m)
        ob = (e0 * o0_ref[:, lanes].astype(F32) + e1 * o1[sl] + e2 * o2[sl]) / (e0 + e1 + e2)
        merged.append(ob.astype(BF16))
    na = oa_ref.shape[1] // CHUNK
    y = jnp.dot(oa_ref[...], _rows(wo_s, 0, na), preferred_element_type=F32)
    return x + y + jnp.dot(jnp.concatenate(merged, axis=1), _rows(wo_s, na), preferred_element_type=F32)


def _ple_stage(x, g_ref, wg_s, p_ref, wp_s):
    xn = _rms(x, g_ref[...]).astype(BF16)
    gate = jax.nn.sigmoid(jnp.dot(xn, _rows(wg_s), preferred_element_type=F32))
    proj = jnp.dot(p_ref[...].astype(BF16), _rows(wp_s), preferred_element_type=F32)
    return x + gate * proj


def _tail_stages(x, g2_ref, gp_ref, p_ref, gf_ref, ffn_s, wpg_s, wpp_s, a_ref, final):
    x = _ffn_stage(x, g2_ref, *ffn_s, a_ref)
    x = _ple_stage(x, gp_ref, wpg_s, p_ref, wpp_s)
    return _rms(x, gf_ref[...]) if final else x


def _tail_specs(ffn, ple, p, g_final, layer, tm, tile):
    g_ple, w_gate, w_proj = ple
    return (_ffn_specs(ffn, layer)
            + [_layer_spec(g_ple, layer), _chunk_spec(w_gate, layer, 1),
               pl.BlockSpec((None, tm, p.shape[-1]), lambda i: (layer, tile(i), 0)),
               _chunk_spec(w_proj, layer, 1), _resident(g_final.shape, lambda i: (0, 0))])


def _tail_scratch(ffn, ple):
    _, w_gate, w_proj = ple
    return _ffn_scratch(ffn) + [_chunk_scratch(w_gate, 1), _chunk_scratch(w_proj, 1)]


def _tail_staging(ffn, ple, refs, scratch):
    _, w_gate, w_proj = ple
    return _ffn_staging(ffn, refs[:3], scratch[:3]) + [
        (refs[3], scratch[3], 0, _n_chunks(w_gate, 1)), (refs[4], scratch[4], 0, _n_chunks(w_proj, 1))]


def _post_ab_kernel(h_ref, oa_ref, o0_ref, l0_ref, o1_ref, l1_ref, o2_ref, l2_ref, woa_ref, wo_ref,
                    g2_ref, wg_ref, wu_ref, wd_ref, gp_ref, wpg_ref, p_ref, wpp_ref, gf_ref, out_ref,
                    wo_s, wg_s, wu_s, wd_s, wpg_s, wpp_s, a_ref, buf_ref, *, n_stage, staging, final):
    i = pl.program_id(0)

    @pl.when(i < n_stage)
    def _():
        _stage_chunks(i, staging((woa_ref, wo_ref, wg_ref, wu_ref, wd_ref, wpg_ref, wpp_ref),
                                 (wo_s, wo_s, wg_s, wu_s, wd_s, wpg_s, wpp_s)))

    @pl.when(i >= n_stage)
    def _():
        x = _outproj_ab_stage(h_ref[...], oa_ref, o0_ref, l0_ref, o1_ref, l1_ref, o2_ref, l2_ref,
                              wo_s, buf_ref)
        out_ref[...] = _tail_stages(x, g2_ref, gp_ref, p_ref, gf_ref, (wg_s, wu_s, wd_s), wpg_s,
                                    wpp_s, a_ref, final)


def _post_ab(h, oa, obs, lses, w_oa, w_out, ffn, ple, p, g_final, layer, j, bsz, *, final, tm=512):
    t, d = h.shape
    s = t // bsz
    nts = s // tm
    n_oa = _n_chunks(w_oa, 1)
    n_ob = _n_chunks(w_out, 1) - n_oa
    n_stage = _n_chunks(ffn[1], 2)
    tile = lambda i: jnp.maximum(i - n_stage, 0)
    row = lambda i: (tile(i), 0)
    res = lambda i: (tile(i) // nts, 0, tile(i) % nts, 0)
    nb = obs[0].shape[-1]
    in_specs = [pl.BlockSpec((tm, d), row), pl.BlockSpec((tm, oa.shape[1]), row),
                pl.BlockSpec((tm, nb), row), pl.BlockSpec((tm, nb), row)]
    args = [h, oa, obs[0], lses[0]]
    for (_, dil), o, lse in zip(B_PAIRS[1:], obs[1:], lses[1:]):
        in_specs += [pl.BlockSpec((None, dil, tm // dil, nb), res)] * 2
        args += [o, lse]
    wo_b_spec = pl.BlockSpec((None, CHUNK, d), lambda i: (j, n_oa + jnp.minimum(i, n_ob - 1), 0),
                             pipeline_mode=pl.Buffered(1))
    in_specs += [_chunk_spec(w_oa, j, 1), wo_b_spec] + _tail_specs(ffn, ple, p, g_final, layer, tm, tile)
    g_ple, w_gate, w_proj = ple

    def staging(refs, scratch):
        return [(refs[0], scratch[0], 0, n_oa), (refs[1], scratch[1], n_oa, n_ob)] + _tail_staging(
            ffn, ple, refs[2:], scratch[2:])

    return pl.pallas_call(
        functools.partial(_post_ab_kernel, n_stage=n_stage, staging=staging, final=final),
        grid=(n_stage + t // tm,),
        in_specs=in_specs,
        out_specs=pl.BlockSpec((tm, d), row),
        out_shape=jax.ShapeDtypeStruct((t, d), F32),
        scratch_shapes=[_chunk_scratch(w_out, 1)] + _tail_scratch(ffn, ple)
        + [pltpu.VMEM((tm, ffn[1].shape[-1]), BF16), pltpu.VMEM((4, nb // LANES, tm, LANES), F32)],
        compiler_params=_params(1),
        name="post_ab",
    )(*args, w_oa, w_out, *ffn, g_ple, w_gate, p, w_proj, g_final)


def _inproj_c_stage(x, g_ref, w_s, q_ref, k_ref, v_ref):
    xn = _rms(x, g_ref[...]).astype(BF16)
    width = q_ref.shape[1]
    scale = HEAD_DIM ** -0.5
    tn = CHUNK
    for c in range(w_s.shape[0]):
        col = c * tn
        y = jnp.dot(xn, w_s[c], preferred_element_type=F32)
        if col < width:
            q_ref[:, col:col + tn] = (y * scale).astype(BF16)
        elif col < 2 * width:
            k_ref[:, col - width:col - width + tn] = y.astype(BF16)
        else:
            v_ref[:, col - 2 * width:col - 2 * width + tn] = y.astype(BF16)


def _pre_c_kernel(h_ref, g1_ref, wg_ref, wu_ref, wd_ref, gm_ref, w_ref, h1_ref, q_ref, k_ref, v_ref,
                  wg_s, wu_s, wd_s, w_s, a_ref, *, n_stage, staging):
    i = pl.program_id(0)

    @pl.when(i < n_stage)
    def _():
        _stage_chunks(i, staging((wg_ref, wu_ref, wd_ref, w_ref), (wg_s, wu_s, wd_s, w_s)))

    @pl.when(i >= n_stage)
    def _():
        x = _ffn_stage(h_ref[...], g1_ref, wg_s, wu_s, wd_s, a_ref)
        h1_ref[...] = x
        _inproj_c_stage(x, gm_ref, w_s, q_ref, k_ref, v_ref)


def _pre_c(h, ffn, g_mix, w_in, layer, j, *, tm=512):
    t, d = h.shape
    n = w_in.shape[-1]
    n_in = _n_chunks(w_in, 2)
    n_stage = max(n_in, _n_chunks(ffn[1], 2))
    row = lambda i: (jnp.maximum(i - n_stage, 0), 0)

    def staging(refs, scratch):
        return _ffn_staging(ffn, refs[:3], scratch[:3]) + [(refs[3], scratch[3], 0, n_in)]

    return pl.pallas_call(
        functools.partial(_pre_c_kernel, n_stage=n_stage, staging=staging),
        grid=(n_stage + t // tm,),
        in_specs=[pl.BlockSpec((tm, d), row)] + _ffn_specs(ffn, layer)
        + [_layer_spec(g_mix, layer), _chunk_spec(w_in, j, 2)],
        out_specs=[pl.BlockSpec((tm, d), row)] + [pl.BlockSpec((tm, n // 3), row)] * 3,
        out_shape=[jax.ShapeDtypeStruct((t, d), F32)] + [jax.ShapeDtypeStruct((t, n // 3), BF16)] * 3,
        scratch_shapes=_ffn_scratch(ffn) + [_chunk_scratch(w_in, 2),
                                            pltpu.VMEM((tm, ffn[1].shape[-1]), BF16)],
        compiler_params=_params(1),
        name="pre_c",
    )(h, *ffn, g_mix, w_in)


BIAS_HEADS = 4


def _na_bias_kernel(rpb_ref, o_ref, *, kh):
    n_heads, n_dr = rpb_ref.shape[:2]
    j = lax.broadcasted_iota(jnp.int32, (GRID_W, LANES), 0)
    lane = lax.broadcasted_iota(jnp.int32, (GRID_W, LANES), 1)
    right = lane >= GRID_W
    c = lane & (GRID_W - 1)
    start = jnp.clip(j - NA_KW // 2, 0, GRID_W - NA_KW)
    inside = (c >= start) & (c < start + NA_KW)
    shift = LANES - (NA_KW - 1)
    for h in range(n_heads):
        skew = lambda dr, extra: pltpu.roll(
            jnp.broadcast_to(rpb_ref[h, dr:dr + 1, :], (GRID_W, LANES)),
            (shift + extra) % LANES, 1, stride=1, stride_axis=0)
        pairs = [jnp.where(inside, jnp.where(right, skew(dr + 1, GRID_W), skew(dr, 0)), NEG_INF)
                 for dr in range(n_dr - 1)]
        for off in range(o_ref.shape[0]):
            for a in range(0, kh, 2):
                o_ref[off, h, :, a * GRID_W:(a + 2) * GRID_W] = pairs[off + a]


def _na_bias(rpb, kh):
    nh, n_dr, n_dc = rpb.shape
    rows = jnp.pad(rpb, ((0, 0), (0, 0), (0, LANES - n_dc)), constant_values=NEG_INF)
    return pl.pallas_call(
        functools.partial(_na_bias_kernel, kh=kh),
        grid=(nh // BIAS_HEADS,),
        in_specs=[pl.BlockSpec((BIAS_HEADS, n_dr, LANES), lambda h: (h, 0, 0))],
        out_specs=pl.BlockSpec((NA_KH, BIAS_HEADS, GRID_W, kh * GRID_W), lambda h: (0, h, 0, 0)),
        out_shape=jax.ShapeDtypeStruct((NA_KH, nh, GRID_W, kh * GRID_W), F32),
        compiler_params=_params(1),
        name="na_bias",
    )(rows)


def _na_kernel(q_ref, k_ref, v_ref, tb_ref, o_ref, *, rows_per_step, rows, kh):
    ig = pl.program_id(2)
    left = lax.broadcasted_iota(jnp.int32, (GRID_W, LANES), 1) < HEAD_DIM
    zero = jnp.zeros((GRID_W, LANES), BF16)

    windows, scores = [], []
    for g in range(rows_per_step):
        i = ig * rows_per_step + g
        rs = jnp.clip(i - kh // 2, 0, rows - kh)
        off = rs - i + (NA_KH - 1)
        win = pl.ds(pl.multiple_of(rs * GRID_W, GRID_W), kh * GRID_W)
        qi = q_ref[g * GRID_W:(g + 1) * GRID_W, :]
        q2 = jnp.concatenate([jnp.where(left, qi, zero), jnp.where(left, zero, qi)], axis=0)
        s = lax.dot_general(q2, k_ref[win, :], (((1,), (1,)), ((), ())), preferred_element_type=F32)
        scores.append(s + tb_ref[off])
        windows.append(win)
    probs, dens = [], []
    for s in scores:
        e = jnp.exp(s - jnp.max(s, axis=-1, keepdims=True))
        dens.append(jnp.sum(e, axis=-1, keepdims=True))
        probs.append(e.astype(BF16))
    for g in range(rows_per_step):
        o2 = jnp.dot(probs[g], v_ref[windows[g], :], preferred_element_type=F32) / dens[g]
        o_ref[g * GRID_W:(g + 1) * GRID_W, :] = jnp.where(
            left, o2[:GRID_W], o2[GRID_W:]).astype(o_ref.dtype)


def _neighbourhood_attention(q, k, v, bias, *, rows_per_step=64):
    bsz, s, width = q.shape
    rows = s // GRID_W
    kh = min(NA_KH, rows)
    n_pairs = width // LANES
    g = rows_per_step
    return pl.pallas_call(
        functools.partial(_na_kernel, rows_per_step=g, rows=rows, kh=kh),
        grid=(bsz, n_pairs, rows // g),
        in_specs=[
            pl.BlockSpec((None, g * GRID_W, LANES), lambda b, p, i: (b, i, p)),
            pl.BlockSpec((None, s, LANES), lambda b, p, i: (b, 0, p)),
            pl.BlockSpec((None, s, LANES), lambda b, p, i: (b, 0, p)),
            pl.BlockSpec((bias.shape[0], None) + bias.shape[2:], lambda b, p, i: (0, p, 0, 0)),
        ],
        out_specs=pl.BlockSpec((None, g * GRID_W, LANES), lambda b, p, i: (b, i, p)),
        out_shape=jax.ShapeDtypeStruct((bsz, s, width), BF16),
        compiler_params=_params(3),
        name="na_attention",
    )(q, k, v, bias)


def _post_c_kernel(h_ref, o_ref, wo_ref, g2_ref, wg_ref, wu_ref, wd_ref, gp_ref, wpg_ref, p_ref,
                   wpp_ref, gf_ref, out_ref, wo_s, wg_s, wu_s, wd_s, wpg_s, wpp_s, a_ref, *,
                   n_stage, staging, final):
    i = pl.program_id(0)

    @pl.when(i < n_stage)
    def _():
        _stage_chunks(i, staging((wo_ref, wg_ref, wu_ref, wd_ref, wpg_ref, wpp_ref),
                                 (wo_s, wg_s, wu_s, wd_s, wpg_s, wpp_s)))

    @pl.when(i >= n_stage)
    def _():
        x = h_ref[...] + jnp.dot(o_ref[...], _rows(wo_s), preferred_element_type=F32)
        out_ref[...] = _tail_stages(x, g2_ref, gp_ref, p_ref, gf_ref, (wg_s, wu_s, wd_s), wpg_s,
                                    wpp_s, a_ref, final)


def _post_c(h, o, w_out, ffn, ple, p, g_final, layer, j, *, final, tm=512):
    t, d = h.shape
    n_stage = _n_chunks(ffn[1], 2)
    tile = lambda i: jnp.maximum(i - n_stage, 0)
    row = lambda i: (tile(i), 0)
    g_ple, w_gate, w_proj = ple

    def staging(refs, scratch):
        return [(refs[0], scratch[0], 0, _n_chunks(w_out, 1))] + _tail_staging(
            ffn, ple, refs[1:], scratch[1:])

    return pl.pallas_call(
        functools.partial(_post_c_kernel, n_stage=n_stage, staging=staging, final=final),
        grid=(n_stage + t // tm,),
        in_specs=[pl.BlockSpec((tm, d), row), pl.BlockSpec((tm, o.shape[1]), row),
                  _chunk_spec(w_out, j, 1)] + _tail_specs(ffn, ple, p, g_final, layer, tm, tile),
        out_specs=pl.BlockSpec((tm, d), row),
        out_shape=jax.ShapeDtypeStruct((t, d), F32),
        scratch_shapes=[_chunk_scratch(w_out, 1)] + _tail_scratch(ffn, ple)
        + [pltpu.VMEM((tm, ffn[1].shape[-1]), BF16)],
        compiler_params=_params(1),
        name="post_c",
    )(h, o, w_out, *ffn, g_ple, w_gate, p, w_proj, g_final)


def _rope_tables(s):
    half = HEAD_DIM // 2
    inv = ROPE_THETA ** (-jnp.arange(half, dtype=F32) / half)
    ang = jnp.arange(s, dtype=jnp.int32).astype(F32)[:, None] * inv[None, :]
    cos, sin = jnp.cos(ang), jnp.sin(ang)
    reps = LANES // HEAD_DIM
    return jnp.concatenate([jnp.tile(jnp.concatenate([cos, cos], axis=1), (1, reps)),
                            jnp.tile(jnp.concatenate([-sin, sin], axis=1), (1, reps))], axis=1)


def _permute_a_heads(w_in, w_out, sink):
    heads = lambda w, axis: jnp.concatenate(
        [lax.slice_in_dim(w, h * HEAD_DIM, (h + 1) * HEAD_DIM, axis=axis) for h in A_HEAD_ORDER],
        axis=axis)
    return heads(w_in, 2), heads(w_out, 1), sink[:, jnp.array(A_HEAD_ORDER)]


def _ab_attention(ab, b1, b2, sink, bsz, s):
    ab = ab.reshape(bsz, 1, s, ab.shape[-1])
    kv_w = 2 * LANES
    first_kv = _A_Q // kv_w
    oa = _banded_attention(ab, ab, ab, q_col=0, k_col=first_kv, v_col=first_kv + 1, radius=A_RADIUS,
                           qpk=A_GROUP, sink=sink, name="band_a")[0]
    obs, lses = [], []
    for gi, (qkv, (window, dil)) in enumerate(zip((ab, b1, b2), B_PAIRS)):
        first = first_kv + 2 if gi == 0 else 0
        o, lse = _banded_attention(qkv, qkv, qkv, q_col=first, k_col=first + 1, v_col=first + 2,
                                   radius=window // (2 * dil), qpk=1, want_lse=True,
                                   name=f"band_b{gi}")
        obs.append(o)
        lses.append(lse)
    flat = lambda a: a.reshape(bsz * s, a.shape[-1])
    obs[0], lses[0] = flat(obs[0]), flat(lses[0])
    return flat(oa), obs, lses


def _c_attention(q, k, v, rpb, bsz, s):
    kh = min(NA_KH, s // GRID_W)
    bias = _na_bias(rpb, kh)
    bias = bias.reshape(NA_KH, bias.shape[1] // 2, 2 * GRID_W, kh * GRID_W)
    o = _neighbourhood_attention(q.reshape(bsz, s, -1), k.reshape(bsz, s, -1),
                                 v.reshape(bsz, s, -1), bias)
    return o.reshape(bsz * s, -1)


def kernel(x, p, norm_ffn1, ffn1_w_gate, ffn1_w_up, ffn1_w_down, norm_mix, w_in_ab, sink_a, w_out_ab, w_in_c, rpb_c, w_out_c, norm_ffn2, ffn2_w_gate, ffn2_w_up, ffn2_w_down, norm_ple, w_ple_gate, w_ple_proj, norm_final):
    bsz, s, d = x.shape
    depth = p.shape[0]
    gains = lambda g: g.reshape(g.shape[0], 1, d)
    ffn1 = (gains(norm_ffn1), ffn1_w_gate, ffn1_w_up, ffn1_w_down)
    ffn2 = (gains(norm_ffn2), ffn2_w_gate, ffn2_w_up, ffn2_w_down)
    ple = (gains(norm_ple), w_ple_gate, w_ple_proj)
    g_mix = gains(norm_mix)
    w_qa, w_oa, sink_a = _permute_a_heads(w_in_ab, w_out_ab, sink_a)
    p2 = p.reshape(depth, bsz * s, p.shape[-1])
    g_final = norm_final.reshape(1, d)
    rope_tab = _rope_tables(s)

    h = x.reshape(bsz * s, d)
    for i in range(depth):
        j = i // 2
        final = i == depth - 1
        if i % 2 == 0:
            h, *proj = _pre_ab(h, ffn1, g_mix, w_qa, w_in_ab, rope_tab, i, j, bsz)
            oa, obs, lses = _ab_attention(*proj, sink_a[j], bsz, s)
            h = _post_ab(h, oa, obs, lses, w_oa, w_out_ab, ffn2, ple, p2, g_final, i, j, bsz,
                         final=final)
        else:
            h, q, k, v = _pre_c(h, ffn1, g_mix, w_in_c, i, j)
            o = _c_attention(q, k, v, rpb_c[j], bsz, s)
            h = _post_c(h, o, w_out_c, ffn2, ple, p2, g_final, i, j, final=final)
    return h.reshape(bsz, s, d)
```

```python
import functools

import jax
import jax.numpy as jnp
from jax import lax
from jax.experimental import pallas as pl
from jax.experimental.pallas import tpu as pltpu

HEAD_DIM = 64
GRID_W = 64
ROPE_THETA = 10000.0
RMS_EPS = 1e-6
A_HEADS = 12
A_KV_HEADS = 4
A_RADIUS = 128
B_PAIRS = ((128, 1), (512, 4), (2048, 16))
B_SLOTS = 4
NA_KH = 8
NA_KW = 16
NEG_INF = -1e30

LANES = 128
ATT_BLOCK = 128
VMEM_LIMIT = 56 * 1024 * 1024
F32 = jnp.float32
BF16 = jnp.bfloat16


def _params(n_axes):
    return pltpu.CompilerParams(dimension_semantics=("arbitrary",) * n_axes,
                                vmem_limit_bytes=VMEM_LIMIT)


def _resident(block_shape, index_map):
    return pl.BlockSpec(block_shape, index_map, pipeline_mode=pl.Buffered(1))


def _rms(x, g):
    ms = jnp.mean(x * x, axis=-1, keepdims=True)
    return x * lax.rsqrt(ms + RMS_EPS) * g


CHUNK = 256
MANY_CHUNKS = 4


def _n_chunks(w, axis):
    return w.shape[axis] // CHUNK


def _chunk_spec(w, idx, axis, block_of=lambda c: c):
    n = _n_chunks(w, axis)
    blk = lambda i: block_of(jnp.minimum(i, n - 1))
    mode = pl.Buffered(2 if n > MANY_CHUNKS else 1)
    if axis == 1:
        return pl.BlockSpec((None, CHUNK, w.shape[2]), lambda i: (idx, blk(i), 0), pipeline_mode=mode)
    return pl.BlockSpec((None, w.shape[1], CHUNK), lambda i: (idx, 0, blk(i)), pipeline_mode=mode)


def _chunk_scratch(w, axis, n=None):
    n = _n_chunks(w, axis) if n is None else n
    shape = (n, CHUNK, w.shape[2]) if axis == 1 else (n, w.shape[1], CHUNK)
    return pltpu.VMEM(shape, BF16)


def _stage_chunks(i, pairs):
    for slab_ref, dst_ref, first, n in pairs:
        dst_ref[first + jnp.minimum(i, n - 1)] = slab_ref[...].astype(BF16)


def _rows(chunks_ref, lo=0, hi=None):
    w = chunks_ref[lo:hi]
    return w.reshape(w.shape[0] * w.shape[1], w.shape[2])


def _ffn_stage(x, g_ref, wg_s, wu_s, wd_s, a_ref):
    xn = _rms(x, g_ref[...]).astype(BF16)
    acc = None
    for f in range(wg_s.shape[0]):
        gate = jnp.dot(xn, wg_s[f], preferred_element_type=F32)
        up = jnp.dot(xn, wu_s[f], preferred_element_type=F32)
        a = (gate * jax.nn.sigmoid(gate) * up).astype(BF16)
        part = jnp.dot(a, wd_s[f], preferred_element_type=F32)
        acc = part if acc is None else acc + part
    return x + 0.5 * acc


def _layer_spec(w, idx):
    tail = w.shape[1:]
    return _resident((None,) + tail, lambda i: (idx,) + (0,) * len(tail))


def _ffn_specs(ffn, layer):
    gain, w_gate, w_up, w_down = ffn
    return [_layer_spec(gain, layer), _chunk_spec(w_gate, layer, 2), _chunk_spec(w_up, layer, 2),
            _chunk_spec(w_down, layer, 1)]


def _ffn_scratch(ffn):
    _, w_gate, w_up, w_down = ffn
    return [_chunk_scratch(w_gate, 2), _chunk_scratch(w_up, 2), _chunk_scratch(w_down, 1)]


def _ffn_staging(ffn, refs, scratch):
    _, w_gate, w_up, w_down = ffn
    ns = (_n_chunks(w_gate, 2), _n_chunks(w_up, 2), _n_chunks(w_down, 1))
    return [(r, s, 0, n) for r, s, n in zip(refs, scratch, ns)]


def _rope(y, cos, sin_signed):
    lane = lax.broadcasted_iota(jnp.int32, y.shape, 1)
    first_half = (lane & (HEAD_DIM // 2)) == 0
    swapped = jnp.where(first_half, pltpu.roll(y, LANES - HEAD_DIM // 2, 1),
                        pltpu.roll(y, HEAD_DIM // 2, 1))
    return y * cos + swapped * sin_signed


A_GROUP = A_HEADS // A_KV_HEADS
A_HEAD_ORDER = (0, 3, 1, 4, 2, 5, 6, 9, 7, 10, 8, 11)
_A_Q, _A_KV = A_HEADS * HEAD_DIM, A_KV_HEADS * HEAD_DIM
_B_W = B_SLOTS * HEAD_DIM


def _scatter_residues(y, buf_ref, out_ref, dil, col):
    n = y.shape[0] // dil
    slabs = y.shape[1] // LANES
    for sl in range(slabs):
        buf_ref[sl] = y[:, sl * LANES:(sl + 1) * LANES]
    for r in range(dil):
        for sl in range(slabs):
            piece = buf_ref[sl, pl.ds(r, n, stride=dil), :]
            out_ref[r, :, col + sl * LANES:col + (sl + 1) * LANES] = piece.astype(out_ref.dtype)


def _inproj_ab_stage(x, g_ref, w_s, cos_ref, sin_ref, qa_ref, ka_ref, va_ref, b_refs, buf_ref):
    xn = _rms(x, g_ref[...]).astype(BF16)
    cos = cos_ref[...]
    sin = sin_ref[...]
    scale = HEAD_DIM ** -0.5
    b_base = _A_Q + 2 * _A_KV
    tn = CHUNK
    for c in range(w_s.shape[0]):
        col = c * tn
        y = jnp.dot(xn, w_s[c], preferred_element_type=F32)
        if col < b_base:
            is_q, is_v = col < _A_Q, col >= _A_Q + _A_KV
        else:
            part = ((col - b_base) % (3 * _B_W)) // _B_W
            is_q, is_v = part == 0, part == 2
        if not is_v:
            y = jnp.concatenate(
                [_rope(y[:, j:j + LANES], cos, sin) for j in range(0, tn, LANES)], axis=1)
        if is_q:
            y = y * scale
        if col < _A_Q:
            qa_ref[:, col:col + tn] = y.astype(BF16)
        elif col < _A_Q + _A_KV:
            ka_ref[:, col - _A_Q:col - _A_Q + tn] = y.astype(BF16)
        elif col < b_base:
            va_ref[:, col - _A_Q - _A_KV:col - _A_Q - _A_KV + tn] = y.astype(BF16)
        else:
            gi, off = divmod(col - b_base, 3 * _B_W)
            dil = B_PAIRS[gi][1]
            if dil == 1:
                b_refs[gi][:, off:off + tn] = y.astype(BF16)
            else:
                _scatter_residues(y, buf_ref.at[c % 2], b_refs[gi], dil, off)


def _pre_ab_kernel(h_ref, g1_ref, wg_ref, wu_ref, wd_ref, gm_ref, wqa_ref, w_ref, cos_ref, sin_ref,
                   h1_ref, qa_ref, ka_ref, va_ref, b0_ref, b1_ref, b2_ref,
                   wg_s, wu_s, wd_s, w_s, a_ref, buf_ref, *, n_stage, staging):
    i = pl.program_id(0)

    @pl.when(i < n_stage)
    def _():
        _stage_chunks(i, staging((wg_ref, wu_ref, wd_ref, wqa_ref, w_ref), (wg_s, wu_s, wd_s, w_s, w_s)))

    @pl.when(i >= n_stage)
    def _():
        x = _ffn_stage(h_ref[...], g1_ref, wg_s, wu_s, wd_s, a_ref)
        h1_ref[...] = x
        _inproj_ab_stage(x, gm_ref, w_s, cos_ref, sin_ref, qa_ref, ka_ref, va_ref,
                         (b0_ref, b1_ref, b2_ref), buf_ref)


def _ab_source_block(c):
    first_b = (_A_Q + 2 * _A_KV) // CHUNK
    n_groups = len(B_PAIRS)
    g, part = (c - first_b) // 3, (c - first_b) % 3
    return jnp.where(c < first_b, c, first_b + part * n_groups + g)


def _pre_ab(h, ffn, g_mix, w_qa, w_in, cos, sin, layer, j, bsz, *, tm=512):
    t, d = h.shape
    s = t // bsz
    nts = s // tm
    n_qa = _n_chunks(w_qa, 2)
    n_in = _n_chunks(w_in, 2)
    n_stage = max(n_in, _n_chunks(ffn[1], 2))
    tile = lambda i: jnp.maximum(i - n_stage, 0)
    row = lambda i: (tile(i), 0)
    tab = lambda i: (tile(i) % nts, 0)
    res = lambda i: (tile(i) // nts, 0, tile(i) % nts, 0)
    widths = (d, _A_Q, _A_KV, _A_KV, 3 * _B_W)
    out_specs = [pl.BlockSpec((tm, w), row) for w in widths]
    out_shape = [jax.ShapeDtypeStruct((t, w), F32 if k == 0 else BF16) for k, w in enumerate(widths)]
    for _, dil in B_PAIRS[1:]:
        out_specs.append(pl.BlockSpec((None, dil, tm // dil, 3 * _B_W), res))
        out_shape.append(jax.ShapeDtypeStruct((bsz, dil, s // dil, 3 * _B_W), BF16))

    def staging(refs, scratch):
        return _ffn_staging(ffn, refs[:3], scratch[:3]) + [
            (refs[4], scratch[4], 0, n_in), (refs[3], scratch[3], 0, n_qa)]

    return pl.pallas_call(
        functools.partial(_pre_ab_kernel, n_stage=n_stage, staging=staging),
        grid=(n_stage + t // tm,),
        in_specs=[pl.BlockSpec((tm, d), row)] + _ffn_specs(ffn, layer)
        + [_layer_spec(g_mix, layer), _chunk_spec(w_qa, j, 2),
           _chunk_spec(w_in, j, 2, _ab_source_block),
           pl.BlockSpec((tm, LANES), tab), pl.BlockSpec((tm, LANES), tab)],
        out_specs=out_specs,
        out_shape=out_shape,
        scratch_shapes=_ffn_scratch(ffn) + [_chunk_scratch(w_in, 2),
                                            pltpu.VMEM((tm, ffn[1].shape[-1]), BF16),
                                            pltpu.VMEM((2, CHUNK // LANES, tm, LANES), F32)],
        compiler_params=_params(1),
        name="pre_ab",
    )(h, *ffn, g_mix, w_qa, w_in, cos, sin)


def _band_kernel(*refs, radius, qpk, length, use_sink, want_lse, tq, n_sub, unroll):
    refs = list(refs)
    sink_ref = refs.pop(0) if use_sink else None
    q_ref, k_ref, v_ref, o_ref = refs[:4]
    lse_ref = refs[4] if want_lse else None
    win = tq + 2 * radius
    n_kg = k_ref.shape[1] // LANES
    nb = 2 * qpk
    t = pl.program_id(2)
    left = lax.broadcasted_iota(jnp.int32, (tq, LANES), 1) < HEAD_DIM
    zero = jnp.zeros((tq, LANES), BF16)
    rel0 = (lax.broadcasted_iota(jnp.int32, (tq, win), 1)
            - lax.broadcasted_iota(jnp.int32, (tq, win), 0))

    def body(jj, carry):
        jobs = []
        for u in range(unroll):
            sub = jj * unroll + u
            rows = pl.ds(pl.multiple_of(sub * tq, tq), tq)
            qpos0 = (t * n_sub + sub) * tq
            kstart = pl.multiple_of(jnp.clip(qpos0 - radius, 0, length - win), radius)
            bias = jnp.where(jnp.abs(rel0 + (kstart - qpos0)) <= radius, 0.0, NEG_INF)
            jobs += [(rows, pl.ds(kstart, win), bias, kg) for kg in range(n_kg)]
        scores = []
        for rows, keys, bias, kg in jobs:
            blocks = []
            for i in range(qpk):
                p = kg * qpk + i
                qp = q_ref[rows, p * LANES:(p + 1) * LANES]
                blocks += [jnp.where(left, qp, zero), jnp.where(left, zero, qp)]
            scores.append(lax.dot_general(
                jnp.concatenate(blocks, axis=0), k_ref[keys, kg * LANES:(kg + 1) * LANES],
                (((1,), (1,)), ((), ())), preferred_element_type=F32))
        stats = []
        for (rows, keys, bias, kg), s in zip(jobs, scores):
            es, dens, ms = [], [], []
            for b in range(nb):
                sb = s[b * tq:(b + 1) * tq] + bias
                mb = jnp.max(sb, axis=-1, keepdims=True)
                if use_sink:
                    sink = sink_ref[kg * nb + b]
                    mb = jnp.maximum(mb, sink)
                eb = jnp.exp(sb - mb)
                db = jnp.sum(eb, axis=-1, keepdims=True)
                if use_sink:
                    db = db + jnp.exp(sink - mb)
                es.append(eb.astype(BF16))
                dens.append(db)
                ms.append(mb)
            stats.append((jnp.concatenate(es, axis=0), dens, ms))
        for (rows, keys, bias, kg), (e, dens, ms) in zip(jobs, stats):
            o2 = jnp.dot(e, v_ref[keys, kg * LANES:(kg + 1) * LANES], preferred_element_type=F32)
            for i in range(qpk):
                p = kg * qpk + i
                even = o2[(2 * i) * tq:(2 * i + 1) * tq] / dens[2 * i]
                odd = o2[(2 * i + 1) * tq:(2 * i + 2) * tq] / dens[2 * i + 1]
                o_ref[rows, p * LANES:(p + 1) * LANES] = jnp.where(left, even, odd).astype(o_ref.dtype)
                if want_lse:
                    lse_e = ms[2 * i] + jnp.log(dens[2 * i])
                    lse_o = ms[2 * i + 1] + jnp.log(dens[2 * i + 1])
                    lse_ref[rows, p * LANES:(p + 1) * LANES] = jnp.where(
                        left, jnp.broadcast_to(lse_e, (tq, LANES)), jnp.broadcast_to(lse_o, (tq, LANES)))
        return carry

    lax.fori_loop(0, n_sub // unroll, body, 0)


def _banded_attention(q, k, v, *, q_col, k_col, v_col, radius, qpk, sink=None, want_lse=False,
                      name):
    bsz, dil, length, _ = q.shape
    tq = ATT_BLOCK
    kw = 2 * LANES
    qw = kw * qpk
    tile = min(length, 1024)
    n_sub = tile // tq
    in_specs = [
        pl.BlockSpec((None, None, tile, qw), lambda b, r, t: (b, r, t, q_col)),
        pl.BlockSpec((None, None, length, kw), lambda b, r, t: (b, r, 0, k_col)),
        pl.BlockSpec((None, None, length, kw), lambda b, r, t: (b, r, 0, v_col)),
    ]
    args = [q, k, v]
    if sink is not None:
        in_specs.insert(0, pl.BlockSpec(memory_space=pltpu.SMEM))
        args.insert(0, sink)
    out_map = lambda b, r, t: (b, r, t, 0)
    out_specs = [pl.BlockSpec((None, None, tile, qw), out_map)]
    out_shape = [jax.ShapeDtypeStruct((bsz, dil, length, qw), BF16)]
    if want_lse:
        out_specs.append(pl.BlockSpec((None, None, tile, qw), out_map))
        out_shape.append(jax.ShapeDtypeStruct((bsz, dil, length, qw), F32))
    return pl.pallas_call(
        functools.partial(_band_kernel, radius=radius, qpk=qpk, length=length,
                          use_sink=sink is not None, want_lse=want_lse, tq=tq, n_sub=n_sub,
                          unroll=min(n_sub, max(2, 4 // qpk))),
        grid=(bsz, dil, length // tile),
        in_specs=in_specs,
        out_specs=out_specs,
        out_shape=out_shape,
        compiler_params=_params(3),
        name=name,
    )(*args)


def _interleave_residues(src_ref, buf_ref, dil):
    n = src_ref.shape[1]
    slabs = src_ref.shape[2] // LANES
    for r in range(dil):
        x = src_ref[r].astype(F32)
        for sl in range(slabs):
            buf_ref[sl, pl.ds(r, n, stride=dil), :] = x[:, sl * LANES:(sl + 1) * LANES]
    return [buf_ref[sl] for sl in range(slabs)]


def _outproj_ab_stage(x, oa_ref, o0_ref, l0_ref, o1_ref, l1_ref, o2_ref, l2_ref, wo_s, buf_ref):
    dils = [dil for _, dil in B_PAIRS]
    o1 = _interleave_residues(o1_ref, buf_ref.at[0], dils[1])
    l1 = _interleave_residues(l1_ref, buf_ref.at[1], dils[1])
    o2 = _interleave_residues(o2_ref, buf_ref.at[2], dils[2])
    l2 = _interleave_residues(l2_ref, buf_ref.at[3], dils[2])
    merged = []
    for sl in range(len(o1)):
        lanes = slice(sl * LANES, (sl + 1) * LANES)
        l0 = l0_ref[:, lanes]
        m = jnp.maximum(jnp.maximum(l0, l1[sl]), l2[sl])
        e0, e1, e2 = jnp.exp(l0 - m), jnp.exp(l1[sl] - m), jnp.exp(l2[sl] - m)
        ob = (e0 * o0_ref[:, lanes].astype(F32) + e1 * o1[sl] + e2 * o2[sl]) / (e0 + e1 + e2)
        merged.append(ob.astype(BF16))
    na = oa_ref.shape[1] // CHUNK
    y = jnp.dot(oa_ref[...], _rows(wo_s, 0, na), preferred_element_type=F32)
    return x + y + jnp.dot(jnp.concatenate(merged, axis=1), _rows(wo_s, na), preferred_element_type=F32)


def _ple_stage(x, g_ref, wg_s, p_ref, wp_s):
    xn = _rms(x, g_ref[...]).astype(BF16)
    gate = jax.nn.sigmoid(jnp.dot(xn, _rows(wg_s), preferred_element_type=F32))
    proj = jnp.dot(p_ref[...].astype(BF16), _rows(wp_s), preferred_element_type=F32)
    return x + gate * proj


def _tail_stages(x, g2_ref, gp_ref, p_ref, gf_ref, ffn_s, wpg_s, wpp_s, a_ref, final):
    x = _ffn_stage(x, g2_ref, *ffn_s, a_ref)
    x = _ple_stage(x, gp_ref, wpg_s, p_ref, wpp_s)
    return _rms(x, gf_ref[...]) if final else x


def _tail_specs(ffn, ple, p, g_final, layer, tm, tile):
    g_ple, w_gate, w_proj = ple
    return (_ffn_specs(ffn, layer)
            + [_layer_spec(g_ple, layer), _chunk_spec(w_gate, layer, 1),
               pl.BlockSpec((None, tm, p.shape[-1]), lambda i: (layer, tile(i), 0)),
               _chunk_spec(w_proj, layer, 1), _resident(g_final.shape, lambda i: (0, 0))])


def _tail_scratch(ffn, ple):
    _, w_gate, w_proj = ple
    return _ffn_scratch(ffn) + [_chunk_scratch(w_gate, 1), _chunk_scratch(w_proj, 1)]


def _tail_staging(ffn, ple, refs, scratch):
    _, w_gate, w_proj = ple
    return _ffn_staging(ffn, refs[:3], scratch[:3]) + [
        (refs[3], scratch[3], 0, _n_chunks(w_gate, 1)), (refs[4], scratch[4], 0, _n_chunks(w_proj, 1))]


def _post_ab_kernel(h_ref, oa_ref, o0_ref, l0_ref, o1_ref, l1_ref, o2_ref, l2_ref, woa_ref, wo_ref,
                    g2_ref, wg_ref, wu_ref, wd_ref, gp_ref, wpg_ref, p_ref, wpp_ref, gf_ref, out_ref,
                    wo_s, wg_s, wu_s, wd_s, wpg_s, wpp_s, a_ref, buf_ref, *, n_stage, staging, final):
    i = pl.program_id(0)

    @pl.when(i < n_stage)
    def _():
        _stage_chunks(i, staging((woa_ref, wo_ref, wg_ref, wu_ref, wd_ref, wpg_ref, wpp_ref),
                                 (wo_s, wo_s, wg_s, wu_s, wd_s, wpg_s, wpp_s)))

    @pl.when(i >= n_stage)
    def _():
        x = _outproj_ab_stage(h_ref[...], oa_ref, o0_ref, l0_ref, o1_ref, l1_ref, o2_ref, l2_ref,
                              wo_s, buf_ref)
        out_ref[...] = _tail_stages(x, g2_ref, gp_ref, p_ref, gf_ref, (wg_s, wu_s, wd_s), wpg_s,
                                    wpp_s, a_ref, final)


def _post_ab(h, oa, obs, lses, w_oa, w_out, ffn, ple, p, g_final, layer, j, bsz, *, final, tm=512):
    t, d = h.shape
    s = t // bsz
    nts = s // tm
    n_oa = _n_chunks(w_oa, 1)
    n_ob = _n_chunks(w_out, 1) - n_oa
    n_stage = _n_chunks(ffn[1], 2)
    tile = lambda i: jnp.maximum(i - n_stage, 0)
    row = lambda i: (tile(i), 0)
    res = lambda i: (tile(i) // nts, 0, tile(i) % nts, 0)
    nb = obs[0].shape[-1]
    in_specs = [pl.BlockSpec((tm, d), row), pl.BlockSpec((tm, oa.shape[1]), row),
                pl.BlockSpec((tm, nb), row), pl.BlockSpec((tm, nb), row)]
    args = [h, oa, obs[0], lses[0]]
    for (_, dil), o, lse in zip(B_PAIRS[1:], obs[1:], lses[1:]):
        in_specs += [pl.BlockSpec((None, dil, tm // dil, nb), res)] * 2
        args += [o, lse]
    wo_b_spec = pl.BlockSpec((None, CHUNK, d), lambda i: (j, n_oa + jnp.minimum(i, n_ob - 1), 0),
                             pipeline_mode=pl.Buffered(1))
    in_specs += [_chunk_spec(w_oa, j, 1), wo_b_spec] + _tail_specs(ffn, ple, p, g_final, layer, tm, tile)
    g_ple, w_gate, w_proj = ple

    def staging(refs, scratch):
        return [(refs[0], scratch[0], 0, n_oa), (refs[1], scratch[1], n_oa, n_ob)] + _tail_staging(
            ffn, ple, refs[2:], scratch[2:])

    return pl.pallas_call(
        functools.partial(_post_ab_kernel, n_stage=n_stage, staging=staging, final=final),
        grid=(n_stage + t // tm,),
        in_specs=in_specs,
        out_specs=pl.BlockSpec((tm, d), row),
        out_shape=jax.ShapeDtypeStruct((t, d), F32),
        scratch_shapes=[_chunk_scratch(w_out, 1)] + _tail_scratch(ffn, ple)
        + [pltpu.VMEM((tm, ffn[1].shape[-1]), BF16), pltpu.VMEM((4, nb // LANES, tm, LANES), F32)],
        compiler_params=_params(1),
        name="post_ab",
    )(*args, w_oa, w_out, *ffn, g_ple, w_gate, p, w_proj, g_final)


def _inproj_c_stage(x, g_ref, w_s, q_ref, k_ref, v_ref):
    xn = _rms(x, g_ref[...]).astype(BF16)
    width = q_ref.shape[1]
    scale = HEAD_DIM ** -0.5
    tn = CHUNK
    for c in range(w_s.shape[0]):
        col = c * tn
        y = jnp.dot(xn, w_s[c], preferred_element_type=F32)
        if col < width:
            q_ref[:, col:col + tn] = (y * scale).astype(BF16)
        elif col < 2 * width:
            k_ref[:, col - width:col - width + tn] = y.astype(BF16)
        else:
            v_ref[:, col - 2 * width:col - 2 * width + tn] = y.astype(BF16)


def _pre_c_kernel(h_ref, g1_ref, wg_ref, wu_ref, wd_ref, gm_ref, w_ref, h1_ref, q_ref, k_ref, v_ref,
                  wg_s, wu_s, wd_s, w_s, a_ref, *, n_stage, staging):
    i = pl.program_id(0)

    @pl.when(i < n_stage)
    def _():
        _stage_chunks(i, staging((wg_ref, wu_ref, wd_ref, w_ref), (wg_s, wu_s, wd_s, w_s)))

    @pl.when(i >= n_stage)
    def _():
        x = _ffn_stage(h_ref[...], g1_ref, wg_s, wu_s, wd_s, a_ref)
        h1_ref[...] = x
        _inproj_c_stage(x, gm_ref, w_s, q_ref, k_ref, v_ref)


def _pre_c(h, ffn, g_mix, w_in, layer, j, *, tm=512):
    t, d = h.shape
    n = w_in.shape[-1]
    n_in = _n_chunks(w_in, 2)
    n_stage = max(n_in, _n_chunks(ffn[1], 2))
    row = lambda i: (jnp.maximum(i - n_stage, 0), 0)

    def staging(refs, scratch):
        return _ffn_staging(ffn, refs[:3], scratch[:3]) + [(refs[3], scratch[3], 0, n_in)]

    return pl.pallas_call(
        functools.partial(_pre_c_kernel, n_stage=n_stage, staging=staging),
        grid=(n_stage + t // tm,),
        in_specs=[pl.BlockSpec((tm, d), row)] + _ffn_specs(ffn, layer)
        + [_layer_spec(g_mix, layer), _chunk_spec(w_in, j, 2)],
        out_specs=[pl.BlockSpec((tm, d), row)] + [pl.BlockSpec((tm, n // 3), row)] * 3,
        out_shape=[jax.ShapeDtypeStruct((t, d), F32)] + [jax.ShapeDtypeStruct((t, n // 3), BF16)] * 3,
        scratch_shapes=_ffn_scratch(ffn) + [_chunk_scratch(w_in, 2),
                                            pltpu.VMEM((tm, ffn[1].shape[-1]), BF16)],
        compiler_params=_params(1),
        name="pre_c",
    )(h, *ffn, g_mix, w_in)


BIAS_HEADS = 4


def _na_bias_kernel(rpb_ref, o_ref, *, kh):
    n_heads, n_dr = rpb_ref.shape[:2]
    j = lax.broadcasted_iota(jnp.int32, (GRID_W, LANES), 0)
    lane = lax.broadcasted_iota(jnp.int32, (GRID_W, LANES), 1)
    right = lane >= GRID_W
    c = lane & (GRID_W - 1)
    start = jnp.clip(j - NA_KW // 2, 0, GRID_W - NA_KW)
    inside = (c >= start) & (c < start + NA_KW)
    shift = LANES - (NA_KW - 1)
    for h in range(n_heads):
        skew = lambda dr, extra: pltpu.roll(
            jnp.broadcast_to(rpb_ref[h, dr:dr + 1, :], (GRID_W, LANES)),
            (shift + extra) % LANES, 1, stride=1, stride_axis=0)
        pairs = [jnp.where(inside, jnp.where(right, skew(dr + 1, GRID_W), skew(dr, 0)), NEG_INF)
                 for dr in range(n_dr - 1)]
        for off in range(o_ref.shape[0]):
            for a in range(0, kh, 2):
                o_ref[off, h, :, a * GRID_W:(a + 2) * GRID_W] = pairs[off + a]


def _na_bias(rpb, kh):
    nh, n_dr, n_dc = rpb.shape
    rows = jnp.pad(rpb, ((0, 0), (0, 0), (0, LANES - n_dc)), constant_values=NEG_INF)
    return pl.pallas_call(
        functools.partial(_na_bias_kernel, kh=kh),
        grid=(nh // BIAS_HEADS,),
        in_specs=[pl.BlockSpec((BIAS_HEADS, n_dr, LANES), lambda h: (h, 0, 0))],
        out_specs=pl.BlockSpec((NA_KH, BIAS_HEADS, GRID_W, kh * GRID_W), lambda h: (0, h, 0, 0)),
        out_shape=jax.ShapeDtypeStruct((NA_KH, nh, GRID_W, kh * GRID_W), F32),
        compiler_params=_params(1),
        name="na_bias",
    )(rows)


def _na_kernel(q_ref, k_ref, v_ref, tb_ref, o_ref, *, rows_per_step, rows, kh):
    ig = pl.program_id(2)
    left = lax.broadcasted_iota(jnp.int32, (GRID_W, LANES), 1) < HEAD_DIM
    zero = jnp.zeros((GRID_W, LANES), BF16)

    windows, scores = [], []
    for g in range(rows_per_step):
        i = ig * rows_per_step + g
        rs = jnp.clip(i - kh // 2, 0, rows - kh)
        off = rs - i + (NA_KH - 1)
        win = pl.ds(pl.multiple_of(rs * GRID_W, GRID_W), kh * GRID_W)
        qi = q_ref[g * GRID_W:(g + 1) * GRID_W, :]
        q2 = jnp.concatenate([jnp.where(left, qi, zero), jnp.where(left, zero, qi)], axis=0)
        s = lax.dot_general(q2, k_ref[win, :], (((1,), (1,)), ((), ())), preferred_element_type=F32)
        scores.append(s + tb_ref[off])
        windows.append(win)
    probs, dens = [], []
    for s in scores:
        e = jnp.exp(s - jnp.max(s, axis=-1, keepdims=True))
        dens.append(jnp.sum(e, axis=-1, keepdims=True))
        probs.append(e.astype(BF16))
    for g in range(rows_per_step):
        o2 = jnp.dot(probs[g], v_ref[windows[g], :], preferred_element_type=F32) / dens[g]
        o_ref[g * GRID_W:(g + 1) * GRID_W, :] = jnp.where(
            left, o2[:GRID_W], o2[GRID_W:]).astype(o_ref.dtype)


def _neighbourhood_attention(q, k, v, bias, *, rows_per_step=64):
    bsz, s, width = q.shape
    rows = s // GRID_W
    kh = min(NA_KH, rows)
    n_pairs = width // LANES
    g = rows_per_step
    return pl.pallas_call(
        functools.partial(_na_kernel, rows_per_step=g, rows=rows, kh=kh),
        grid=(bsz, n_pairs, rows // g),
        in_specs=[
            pl.BlockSpec((None, g * GRID_W, LANES), lambda b, p, i: (b, i, p)),
            pl.BlockSpec((None, s, LANES), lambda b, p, i: (b, 0, p)),
            pl.BlockSpec((None, s, LANES), lambda b, p, i: (b, 0, p)),
            pl.BlockSpec((bias.shape[0], None) + bias.shape[2:], lambda b, p, i: (0, p, 0, 0)),
        ],
        out_specs=pl.BlockSpec((None, g * GRID_W, LANES), lambda b, p, i: (b, i, p)),
        out_shape=jax.ShapeDtypeStruct((bsz, s, width), BF16),
        compiler_params=_params(3),
        name="na_attention",
    )(q, k, v, bias)


def _post_c_kernel(h_ref, o_ref, wo_ref, g2_ref, wg_ref, wu_ref, wd_ref, gp_ref, wpg_ref, p_ref,
                   wpp_ref, gf_ref, out_ref, wo_s, wg_s, wu_s, wd_s, wpg_s, wpp_s, a_ref, *,
                   n_stage, staging, final):
    i = pl.program_id(0)

    @pl.when(i < n_stage)
    def _():
        _stage_chunks(i, staging((wo_ref, wg_ref, wu_ref, wd_ref, wpg_ref, wpp_ref),
                                 (wo_s, wg_s, wu_s, wd_s, wpg_s, wpp_s)))

    @pl.when(i >= n_stage)
    def _():
        x = h_ref[...] + jnp.dot(o_ref[...], _rows(wo_s), preferred_element_type=F32)
        out_ref[...] = _tail_stages(x, g2_ref, gp_ref, p_ref, gf_ref, (wg_s, wu_s, wd_s), wpg_s,
                                    wpp_s, a_ref, final)


def _post_c(h, o, w_out, ffn, ple, p, g_final, layer, j, *, final, tm=512):
    t, d = h.shape
    n_stage = _n_chunks(ffn[1], 2)
    tile = lambda i: jnp.maximum(i - n_stage, 0)
    row = lambda i: (tile(i), 0)
    g_ple, w_gate, w_proj = ple

    def staging(refs, scratch):
        return [(refs[0], scratch[0], 0, _n_chunks(w_out, 1))] + _tail_staging(
            ffn, ple, refs[1:], scratch[1:])

    return pl.pallas_call(
        functools.partial(_post_c_kernel, n_stage=n_stage, staging=staging, final=final),
        grid=(n_stage + t // tm,),
        in_specs=[pl.BlockSpec((tm, d), row), pl.BlockSpec((tm, o.shape[1]), row),
                  _chunk_spec(w_out, j, 1)] + _tail_specs(ffn, ple, p, g_final, layer, tm, tile),
        out_specs=pl.BlockSpec((tm, d), row),
        out_shape=jax.ShapeDtypeStruct((t, d), F32),
        scratch_shapes=[_chunk_scratch(w_out, 1)] + _tail_scratch(ffn, ple)
        + [pltpu.VMEM((tm, ffn[1].shape[-1]), BF16)],
        compiler_params=_params(1),
        name="post_c",
    )(h, o, w_out, *ffn, g_ple, w_gate, p, w_proj, g_final)


def _rope_tables(s):
    half = HEAD_DIM // 2
    inv = ROPE_THETA ** (-jnp.arange(half, dtype=F32) / half)
    ang = jnp.arange(s, dtype=jnp.int32).astype(F32)[:, None] * inv[None, :]
    cos, sin = jnp.cos(ang), jnp.sin(ang)
    reps = LANES // HEAD_DIM
    return (jnp.tile(jnp.concatenate([cos, cos], axis=1), (1, reps)),
            jnp.tile(jnp.concatenate([-sin, sin], axis=1), (1, reps)))


def _permute_a_heads(w_in, w_out, sink):
    heads = lambda w, axis: jnp.concatenate(
        [lax.slice_in_dim(w, h * HEAD_DIM, (h + 1) * HEAD_DIM, axis=axis) for h in A_HEAD_ORDER],
        axis=axis)
    return heads(w_in, 2), heads(w_out, 1), sink[:, jnp.array(A_HEAD_ORDER)]


def _ab_attention(qa, ka, va, b0, b1, b2, sink, bsz, s):
    seq = lambda a: a.reshape(bsz, 1, s, a.shape[-1])
    oa = _banded_attention(seq(qa), seq(ka), seq(va), q_col=0, k_col=0, v_col=0, radius=A_RADIUS,
                           qpk=A_GROUP, sink=sink, name="band_a")[0]
    obs, lses = [], []
    for gi, (qkv, (window, dil)) in enumerate(zip((seq(b0), b1, b2), B_PAIRS)):
        o, lse = _banded_attention(qkv, qkv, qkv, q_col=0, k_col=1, v_col=2,
                                   radius=window // (2 * dil), qpk=1, want_lse=True,
                                   name=f"band_b{gi}")
        obs.append(o)
        lses.append(lse)
    flat = lambda a: a.reshape(bsz * s, a.shape[-1])
    obs[0], lses[0] = flat(obs[0]), flat(lses[0])
    return flat(oa), obs, lses


def _c_attention(q, k, v, rpb, bsz, s):
    kh = min(NA_KH, s // GRID_W)
    bias = _na_bias(rpb, kh)
    bias = bias.reshape(NA_KH, bias.shape[1] // 2, 2 * GRID_W, kh * GRID_W)
    o = _neighbourhood_attention(q.reshape(bsz, s, -1), k.reshape(bsz, s, -1),
                                 v.reshape(bsz, s, -1), bias)
    return o.reshape(bsz * s, -1)


def kernel(x, p, norm_ffn1, ffn1_w_gate, ffn1_w_up, ffn1_w_down, norm_mix, w_in_ab, sink_a, w_out_ab, w_in_c, rpb_c, w_out_c, norm_ffn2, ffn2_w_gate, ffn2_w_up, ffn2_w_down, norm_ple, w_ple_gate, w_ple_proj, norm_final):
    bsz, s, d = x.shape
    depth = p.shape[0]
    gains = lambda g: g.reshape(g.shape[0], 1, d)
    ffn1 = (gains(norm_ffn1), ffn1_w_gate, ffn1_w_up, ffn1_w_down)
    ffn2 = (gains(norm_ffn2), ffn2_w_gate, ffn2_w_up, ffn2_w_down)
    ple = (gains(norm_ple), w_ple_gate, w_ple_proj)
    g_mix = gains(norm_mix)
    w_qa, w_oa, sink_a = _permute_a_heads(w_in_ab, w_out_ab, sink_a)
    p2 = p.reshape(depth, bsz * s, p.shape[-1])
    g_final = norm_final.reshape(1, d)
    cos, sin = _rope_tables(s)

    h = x.reshape(bsz * s, d)
    for i in range(depth):
        j = i // 2
        final = i == depth - 1
        if i % 2 == 0:
            h, *proj = _pre_ab(h, ffn1, g_mix, w_qa, w_in_ab, cos, sin, i, j, bsz)
            oa, obs, lses = _ab_attention(*proj, sink_a[j], bsz, s)
            h = _post_ab(h, oa, obs, lses, w_oa, w_out_ab, ffn2, ple, p2, g_final, i, j, bsz,
                         final=final)
        else:
            h, q, k, v = _pre_c(h, ffn1, g_mix, w_in_c, i, j)
            o = _c_attention(q, k, v, rpb_c[j], bsz, s)
            h = _post_c(h, o, w_out_c, ffn2, ple, p2, g_final, i, j, final=final)
    return h.reshape(bsz, s, d)
```
